```python
import math
import jax, jax.numpy as jnp
from jax import lax
import numpy as np

D_MODEL = 1024
BATCH = 8
SEQ = 4096
DEPTH = 1

ATTN_HEADS = 4
ATTN_HEAD_DIM = 64
ATTN_V_DIM = 2 * ATTN_HEAD_DIM
ATTN_WIDTH = ATTN_HEADS * ATTN_V_DIM
QK_WIDTH = ATTN_HEADS * 2 * ATTN_HEAD_DIM
ROPE_THETA = 500000.0
ROPE_DIM = ATTN_HEAD_DIM // 4
Q_BLOCK = 128
LRU_WIDTH = D_MODEL - ATTN_WIDTH
LRU_BLOCKS = 8
LRU_BLOCK_DIM = LRU_WIDTH // LRU_BLOCKS
CONV_WIDTH = 4
LRU_C = 8.0
N_DIRS = 2
IN_WIDTH = 2 * QK_WIDTH + ATTN_WIDTH + 2 * LRU_WIDTH
N_GROUPS = 4
EXPERTS_PER_GROUP = 8
N_EXPERTS = N_GROUPS * EXPERTS_PER_GROUP
TOP_K = 2
D_EXPERT = 256
EPS = 1e-6

kernel_name = 'hybrid_diffattn_rglru_hmoe_encoder'


def rmsnorm(x, g):
    xf = x.astype(jnp.float32)
    y = xf * lax.rsqrt(jnp.mean(xf * xf, axis=-1, keepdims=True) + EPS)
    return (y * g.astype(jnp.float32)).astype(x.dtype)


def rope_tables(seq_len):
    pos = jnp.arange(seq_len, dtype=jnp.float32)
    inv_freq = ROPE_THETA ** (-jnp.arange(0, ROPE_DIM, 2, dtype=jnp.float32) / ROPE_DIM)
    ang = pos[:, None] * inv_freq[None, :]
    return jnp.cos(ang), jnp.sin(ang)


def partial_rope(t, cos, sin):
    c = cos[None, :, None, None, :].astype(t.dtype)
    s = sin[None, :, None, None, :].astype(t.dtype)
    half = ROPE_DIM // 2
    t1, t2, rest = t[..., :half], t[..., half:ROPE_DIM], t[..., ROPE_DIM:]
    return jnp.concatenate([t1 * c - t2 * s, t2 * c + t1 * s, rest], axis=-1)


def diff_attention(q, k, v, lam, lambda_init, subln_g):
    B, S = q.shape[0], q.shape[1]
    n_blk = S // Q_BLOCK
    q = q * (ATTN_HEAD_DIM ** -0.5)
    qb = q.reshape(B, n_blk, Q_BLOCK, ATTN_HEADS, 2, ATTN_HEAD_DIM).transpose(1, 0, 2, 3, 4, 5)

    def one_block(q_blk):
        s = jnp.einsum('bqhmd,bkhmd->bhmqk', q_blk, k).astype(jnp.float32)
        p = jax.nn.softmax(s, axis=-1)
        a = p[:, :, 0] - lam * p[:, :, 1]
        return jnp.einsum('bhqk,bkhe->bqhe', a.astype(v.dtype), v)

    o = lax.map(one_block, qb)
    o = o.transpose(1, 0, 2, 3, 4).reshape(B, S, ATTN_HEADS, ATTN_V_DIM)
    o = rmsnorm(o, subln_g) * (1.0 - lambda_init)
    return o.reshape(B, S, ATTN_WIDTH)


def centred_depthwise_conv(x, w, b):
    C = x.shape[-1]
    left = CONV_WIDTH // 2
    y = lax.conv_general_dilated(
        x, w[:, None, :].astype(x.dtype), window_strides=(1,),
        padding=[(left, CONV_WIDTH - 1 - left)],
        dimension_numbers=('NWC', 'WIO', 'NWC'), feature_group_count=C)
    return y + b.astype(x.dtype)


def _linear_recurrence(c1, c2):
    a1, b1 = c1
    a2, b2 = c2
    return a1 * a2, a2 * b1 + b2


def rg_lru(x, w_r, b_r, w_i, b_i, lam_param, reverse):
    B, S, C = x.shape
    xf = x.astype(jnp.float32)
    xb = xf.reshape(B, S, LRU_BLOCKS, LRU_BLOCK_DIM)
    r = jax.nn.sigmoid(jnp.einsum('bshi,hij->bshj', xb, w_r.astype(jnp.float32)).reshape(B, S, C) + b_r)
    i = jax.nn.sigmoid(jnp.einsum('bshi,hij->bshj', xb, w_i.astype(jnp.float32)).reshape(B, S, C) + b_i)
    log_a = -LRU_C * jax.nn.softplus(-lam_param.astype(jnp.float32)) * r
    a = jnp.exp(log_a)
    u = jnp.sqrt(-jnp.expm1(2.0 * log_a)) * (i * xf)
    _, h = lax.associative_scan(_linear_recurrence, (a, u), axis=1, reverse=reverse)
    return h


def hierarchical_moe(h, w_grp, w_exp, w_gate, w_up, w_down):
    B, S, D = h.shape
    t = h.reshape(B * S, D)
    g_prob = jax.nn.softmax((t @ w_grp).astype(jnp.float32), axis=-1)
    g_top_p, g_idx = lax.top_k(g_prob, 1)
    e_logits = (t @ w_exp).astype(jnp.float32).reshape(B * S, N_GROUPS, EXPERTS_PER_GROUP)
    e_in_grp = jnp.take_along_axis(e_logits, g_idx[:, :, None], axis=1)[:, 0]
    e_top_logit, e_idx = lax.top_k(e_in_grp, TOP_K)
    e_w = jax.nn.softmax(e_top_logit, axis=-1) * g_top_p
    global_idx = g_idx * EXPERTS_PER_GROUP + e_idx
    comb = jnp.sum(jax.nn.one_hot(global_idx, N_EXPERTS, dtype=jnp.float32) * e_w[..., None], axis=1)
    comb = comb.astype(t.dtype)
    out = jnp.zeros_like(t)
    for e in range(N_EXPERTS):
        hid = jax.nn.silu(t @ w_gate[e]) * (t @ w_up[e])
        out = out + comb[:, e:e + 1] * (hid @ w_down[e])
    return out.reshape(B, S, D)


def setup_inputs(seed: int = 0) -> dict:
    key = jax.random.key(seed)
    ks = jax.random.split(key, 24)
    f32 = jnp.float32
    nrm = lambda k, shape, scale: jax.random.normal(k, shape, f32) * scale
    a_c = jax.random.uniform(ks[15], (DEPTH, N_DIRS, LRU_WIDTH), f32, 0.9, 0.999)
    a0 = a_c ** (1.0 / LRU_C)
    lru_lambda = jnp.log(a0) - jnp.log1p(-a0)
    return {
        'x': jax.random.normal(ks[0], (BATCH, SEQ, D_MODEL), f32),
        'norm1_g': 1.0 + nrm(ks[1], (DEPTH, D_MODEL), 0.02),
        'w_in': nrm(ks[2], (DEPTH, D_MODEL, IN_WIDTH), D_MODEL ** -0.5),
        'lambda_q1': nrm(ks[3], (DEPTH, ATTN_HEAD_DIM), 0.1),
        'lambda_k1': nrm(ks[4], (DEPTH, ATTN_HEAD_DIM), 0.1),
        'lambda_q2': nrm(ks[5], (DEPTH, ATTN_HEAD_DIM), 0.1),
        'lambda_k2': nrm(ks[6], (DEPTH, ATTN_HEAD_DIM), 0.1),
        'subln_g': 1.0 + nrm(ks[7], (DEPTH, ATTN_V_DIM), 0.02),
        'conv_w': nrm(ks[8], (DEPTH, CONV_WIDTH, LRU_WIDTH), CONV_WIDTH ** -0.5),
        'conv_b': nrm(ks[9], (DEPTH, LRU_WIDTH), 0.01),
        'lru_w_r': nrm(ks[10], (DEPTH, N_DIRS, LRU_BLOCKS, LRU_BLOCK_DIM, LRU_BLOCK_DIM), LRU_BLOCK_DIM ** -0.5),
        'lru_b_r': nrm(ks[11], (DEPTH, N_DIRS, LRU_WIDTH), 0.01),
        'lru_w_i': nrm(ks[12], (DEPTH, N_DIRS, LRU_BLOCKS, LRU_BLOCK_DIM, LRU_BLOCK_DIM), LRU_BLOCK_DIM ** -0.5),
        'lru_b_i': nrm(ks[13], (DEPTH, N_DIRS, LRU_WIDTH), 0.01),
        'lru_lambda': lru_lambda,
        'w_out': nrm(ks[14], (DEPTH, D_MODEL, D_MODEL), D_MODEL ** -0.5),
        'norm2_g': 1.0 + nrm(ks[16], (DEPTH, D_MODEL), 0.02),
        'w_grp': nrm(ks[17], (DEPTH, D_MODEL, N_GROUPS), D_MODEL ** -0.5),
        'w_exp': nrm(ks[18], (DEPTH, D_MODEL, N_EXPERTS), D_MODEL ** -0.5),
        'w_gate': nrm(ks[19], (DEPTH, N_EXPERTS, D_MODEL, D_EXPERT), D_MODEL ** -0.5),
        'w_up': nrm(ks[20], (DEPTH, N_EXPERTS, D_MODEL, D_EXPERT), D_MODEL ** -0.5),
        'w_down': nrm(ks[21], (DEPTH, N_EXPERTS, D_EXPERT, D_MODEL), D_EXPERT ** -0.5),
        'final_g': 1.0 + nrm(ks[22], (D_MODEL,), 0.02),
    }


def reference(x, norm1_g, w_in, lambda_q1, lambda_k1, lambda_q2, lambda_k2, subln_g,
              conv_w, conv_b, lru_w_r, lru_b_r, lru_w_i, lru_b_i, lru_lambda, w_out,
              norm2_g, w_grp, w_exp, w_gate, w_up, w_down, final_g):
    B, S, _ = x.shape
    cos, sin = rope_tables(S)
    for l in range(DEPTH):
        lambda_init = 0.8 - 0.6 * math.exp(-0.3 * l)
        h = rmsnorm(x, norm1_g[l])
        proj = h @ w_in[l]
        q, k, v, xr, gate = jnp.split(
            proj, [QK_WIDTH, 2 * QK_WIDTH, 2 * QK_WIDTH + ATTN_WIDTH,
                   2 * QK_WIDTH + ATTN_WIDTH + LRU_WIDTH], axis=-1)
        q = partial_rope(q.reshape(B, S, ATTN_HEADS, 2, ATTN_HEAD_DIM), cos, sin)
        k = partial_rope(k.reshape(B, S, ATTN_HEADS, 2, ATTN_HEAD_DIM), cos, sin)
        v = v.reshape(B, S, ATTN_HEADS, ATTN_V_DIM)
        lam = (jnp.exp(jnp.sum(lambda_q1[l].astype(jnp.float32) * lambda_k1[l].astype(jnp.float32)))
               - jnp.exp(jnp.sum(lambda_q2[l].astype(jnp.float32) * lambda_k2[l].astype(jnp.float32)))
               + lambda_init)
        attn_out = diff_attention(q, k, v, lam, lambda_init, subln_g[l])
        xc = centred_depthwise_conv(xr, conv_w[l], conv_b[l])
        h_fwd = rg_lru(xc, lru_w_r[l, 0], lru_b_r[l, 0], lru_w_i[l, 0], lru_b_i[l, 0], lru_lambda[l, 0], False)
        h_bwd = rg_lru(xc, lru_w_r[l, 1], lru_b_r[l, 1], lru_w_i[l, 1], lru_b_i[l, 1], lru_lambda[l, 1], True)
        rnn_out = (h_fwd + h_bwd).astype(x.dtype) * jax.nn.gelu(gate)
        mix = jnp.concatenate([attn_out, rnn_out], axis=-1) @ w_out[l]
        x = x + mix
        h = rmsnorm(x, norm2_g[l])
        x = x + hierarchical_moe(h, w_grp[l], w_exp[l], w_gate[l], w_up[l], w_down[l])
    return rmsnorm(x, final_g)
```

```python
import functools
import math

import jax
import jax.numpy as jnp
from jax import lax
from jax.experimental import pallas as pl
from jax.experimental.pallas import tpu as pltpu

F32 = jnp.float32
BF16 = jnp.bfloat16

D_MODEL = 1024
ATTN_HEADS = 4
HEAD_DIM = 64
V_DIM = 2 * HEAD_DIM
ATTN_WIDTH = ATTN_HEADS * V_DIM
ROPE_THETA = 500000.0
ROPE_DIM = HEAD_DIM // 4
LRU_WIDTH = D_MODEL - ATTN_WIDTH
LRU_BLOCK_DIM = 64
CONV_WIDTH = 4
LRU_C = 8.0
N_GROUPS = 4
EXPERTS_PER_GROUP = 8
N_EXPERTS = N_GROUPS * EXPERTS_PER_GROUP
D_EXPERT = 256
EPS = 1e-6
LAMBDA_INIT = 0.8 - 0.6 * math.exp(0.0)

LANES = 128
SUBLANES = 8
MXU_DIM = 256
VMEM_LIMIT = 56 * 1024 * 1024


def _cparams(semantics):
    return pltpu.CompilerParams(dimension_semantics=semantics, vmem_limit_bytes=VMEM_LIMIT)


def _rope_tables(seq_len):
    pos = jnp.arange(seq_len, dtype=F32)
    inv_freq = ROPE_THETA ** (-jnp.arange(0, ROPE_DIM, 2, dtype=F32) / ROPE_DIM)
    ang = pos[:, None] * inv_freq[None, :]
    cos, sin = jnp.cos(ang), jnp.sin(ang)
    half = ROPE_DIM // 2
    j = jnp.arange(LANES) % HEAD_DIM
    f = j % half
    c = jnp.where(j[None, :] < ROPE_DIM, cos[:, f], 1.0)
    s_lo = jnp.where(j[None, :] < half, -sin[:, f], 0.0)
    s_hi = jnp.where((j[None, :] >= half) & (j[None, :] < ROPE_DIM), sin[:, f], 0.0)
    return c, s_lo, s_hi


def _inproj_kernel(x_ref, g_ref, w_ref, c_ref, slo_ref, shi_ref,
                   q_ref, k_ref, v_ref, xr_ref, gate_ref):
    x = x_ref[...]
    ms = jnp.mean(x * x, axis=-1, keepdims=True)
    h = (x * lax.rsqrt(ms + EPS) * g_ref[...]).astype(BF16)
    c, slo, shi = c_ref[...], slo_ref[...], shi_ref[...]

    def rope(p):
        return p * c + pltpu.roll(p, LANES - ROPE_DIM // 2, 1) * slo + pltpu.roll(p, ROPE_DIM // 2, 1) * shi

    q_scale = HEAD_DIM ** -0.5
    for hd in range(ATTN_HEADS):
        lo = hd * V_DIM
        pq = jnp.dot(h, w_ref[:, lo:lo + V_DIM], preferred_element_type=F32)
        q_ref[:, lo:lo + V_DIM] = (rope(pq) * q_scale).astype(BF16)
        pk = jnp.dot(h, w_ref[:, ATTN_WIDTH + lo:ATTN_WIDTH + lo + V_DIM], preferred_element_type=F32)
        k_ref[:, lo:lo + V_DIM] = rope(pk).astype(BF16)
    off = 2 * ATTN_WIDTH
    v_ref[...] = jnp.dot(h, w_ref[:, off:off + ATTN_WIDTH], preferred_element_type=F32).astype(BF16)
    off += ATTN_WIDTH
    xr_ref[...] = jnp.dot(h, w_ref[:, off:off + LRU_WIDTH], preferred_element_type=F32).astype(BF16)
    off += LRU_WIDTH
    gate_ref[...] = jnp.dot(h, w_ref[:, off:off + LRU_WIDTH], preferred_element_type=F32).astype(BF16)


def _inproj(x, g, w_in, tm):
    B, S, D = x.shape
    c, slo, shi = _rope_tables(S)
    row = lambda si, b: (b, si, 0)
    tab = lambda si, b: (si, 0)
    const = lambda si, b: (0, 0)
    out = jax.ShapeDtypeStruct((B, S, ATTN_WIDTH), BF16)
    return pl.pallas_call(
        _inproj_kernel,
        grid=(S // tm, B),
        in_specs=[
            pl.BlockSpec((None, tm, D), row),
            pl.BlockSpec((1, D), const),
            pl.BlockSpec(w_in.shape, const),
            pl.BlockSpec((tm, LANES), tab),
            pl.BlockSpec((tm, LANES), tab),
            pl.BlockSpec((tm, LANES), tab),
        ],
        out_specs=[pl.BlockSpec((None, tm, ATTN_WIDTH), row)] * 5,
        out_shape=[out] * 5,
        compiler_params=_cparams(("arbitrary", "arbitrary")),
        name="inproj",
    )(x, g.reshape(1, D), w_in.astype(BF16), c, slo, shi)


def _attn_kernel(lam_ref, q_ref, k_ref, v_ref, sg_ref, o_ref):
    lv = lam_ref[...]
    lam = (jnp.exp(jnp.sum(lv[0:1] * lv[1:2], axis=-1, keepdims=True))
           - jnp.exp(jnp.sum(lv[2:3] * lv[3:4], axis=-1, keepdims=True)) + LAMBDA_INIT)
    q = q_ref[...]
    k = k_ref[...]
    v = v_ref[...]
    lane = lax.broadcasted_iota(jnp.int32, q.shape, 1)
    zero = jnp.zeros_like(q)

    def softmax_v(qm):
        s = lax.dot_general(qm, k, (((1,), (1,)), ((), ())), preferred_element_type=F32)
        m = jnp.max(s, axis=-1, keepdims=True)
        e = jnp.exp(s - m)
        l = jnp.sum(e, axis=-1, keepdims=True)
        return jnp.dot(e.astype(BF16), v, preferred_element_type=F32) / l

    o = softmax_v(jnp.where(lane < HEAD_DIM, q, zero)) - lam * softmax_v(jnp.where(lane >= HEAD_DIM, q, zero))
    ms = jnp.mean(o * o, axis=-1, keepdims=True)
    o_ref[...] = (o * lax.rsqrt(ms + EPS) * sg_ref[...] * (1.0 - LAMBDA_INIT)).astype(o_ref.dtype)


def _attention(q, k, v, lam_vecs, subln_g, tq):
    B, S, _ = q.shape
    qmap = lambda b, h, i: (b, i, h)
    kmap = lambda b, h, i: (b, 0, h)
    const = lambda b, h, i: (0, 0)
    return pl.pallas_call(
        _attn_kernel,
        grid=(B, ATTN_HEADS, S // tq),
        in_specs=[
            pl.BlockSpec((SUBLANES, LANES), const),
            pl.BlockSpec((None, tq, V_DIM), qmap),
            pl.BlockSpec((None, S, V_DIM), kmap),
            pl.BlockSpec((None, S, V_DIM), kmap),
            pl.BlockSpec((1, V_DIM), const),
        ],
        out_specs=pl.BlockSpec((None, tq, V_DIM), qmap),
        out_shape=jax.ShapeDtypeStruct((B, S, ATTN_WIDTH), BF16),
        compiler_params=_cparams(("arbitrary", "arbitrary", "arbitrary")),
        name="diffattn",
    )(lam_vecs, q, k, v, subln_g.reshape(1, V_DIM))


def _shift_rows(x, d, fill, up):
    n = x.shape[0]
    row = lax.broadcasted_iota(jnp.int32, x.shape, 0)
    if up:
        return jnp.where(row < n - d, pltpu.roll(x, n - d, 0), fill)
    return jnp.where(row >= d, pltpu.roll(x, d, 0), fill)


def _chunk_scan(a, u, reverse):
    n = a.shape[0]
    d = 1
    while d < n:
        a_s = _shift_rows(a, d, 1.0, reverse)
        u_s = _shift_rows(u, d, 0.0, reverse)
        u = a * u_s + u
        a = a * a_s
        d *= 2
    return a, u


def _rglru_kernel(xr_ref, gate_ref, cw_ref, cb_ref, wg_ref, bg_ref, coef_ref, o_ref, xp_ref, hf_ref, *, tc):
    S, C = xr_ref.shape
    pad = SUBLANES
    n_chunks = S // tc
    xp_ref[0:pad, :] = jnp.zeros((pad, C), F32)
    xp_ref[pad + S:pad + S + pad, :] = jnp.zeros((pad, C), F32)
    xp_ref[pad:pad + S, :] = xr_ref[...].astype(F32)
    cw = cw_ref[...]
    cb = cb_ref[...]
    wg = wg_ref[...]
    bg = bg_ref[...]
    coef = coef_ref[...]

    def gates(r0, direction):
        w = xp_ref[pl.ds(r0, tc + 2 * pad), :]
        n = tc + 2 * pad
        xc = (w * cw[2:3] + pltpu.roll(w, 2, 0) * cw[0:1] + pltpu.roll(w, 1, 0) * cw[1:2]
              + pltpu.roll(w, n - 1, 0) * cw[3:4])[pad:pad + tc] + cb
        lin = jnp.dot(xc.astype(BF16), wg[:, 2 * C * direction:2 * C * (direction + 1)],
                      preferred_element_type=F32)
        r = jax.nn.sigmoid(lin[:, :C] + bg[2 * direction:2 * direction + 1])
        i = jax.nn.sigmoid(lin[:, C:] + bg[2 * direction + 1:2 * direction + 2])
        log_a = coef[direction:direction + 1] * r
        a = jnp.exp(log_a)
        th = jnp.tanh(log_a)
        u = jnp.sqrt(-2.0 * th / (1.0 - th)) * (i * xc)
        return a, u

    def fwd(ci, h_prev):
        r0 = pl.multiple_of(ci * tc, tc)
        a, u = gates(r0, 0)
        a_c, h0 = _chunk_scan(a, u, False)
        h = a_c * h_prev + h0
        hf_ref[pl.ds(r0, tc), :] = h
        return jnp.broadcast_to(h[tc - 1:tc], h_prev.shape)

    lax.fori_loop(0, n_chunks, fwd, jnp.zeros((tc, C), F32))

    def bwd(cj, h_next):
        r0 = pl.multiple_of((n_chunks - 1 - cj) * tc, tc)
        a, u = gates(r0, 1)
        a_c, h0 = _chunk_scan(a, u, True)
        h = a_c * h_next + h0
        g = gate_ref[pl.ds(r0, tc), :].astype(F32)
        o_ref[pl.ds(r0, tc), :] = ((hf_ref[pl.ds(r0, tc), :] + h) * jax.nn.gelu(g)).astype(o_ref.dtype)
        return jnp.broadcast_to(h[0:1], h_next.shape)

    lax.fori_loop(0, n_chunks, bwd, jnp.zeros((tc, C), F32))


def _block_diag(w):
    nb, d, _ = w.shape
    eye = jnp.eye(nb, dtype=w.dtype)
    return (eye[:, None, :, None] * w[:, :, None, :]).reshape(nb * d, nb * d)


def _rglru(xr, gate, conv_w, conv_b, w_r, b_r, w_i, b_i, lru_lambda, tc):
    B, S, W = xr.shape
    C = MXU_DIM
    n_c = W // C
    mats = [_block_diag(w_r[0]), _block_diag(w_i[0]), _block_diag(w_r[1]), _block_diag(w_i[1])]
    wg = jnp.stack([jnp.concatenate([m[c * C:(c + 1) * C, c * C:(c + 1) * C] for m in mats], axis=1)
                    for c in range(n_c)]).astype(BF16)
    bg = jnp.stack([b_r[0], b_i[0], b_r[1], b_i[1]])
    coef = -LRU_C * jax.nn.softplus(-lru_lambda.astype(F32))
    blk = lambda b, c: (b, 0, c)
    par = lambda b, c: (0, c)
    return pl.pallas_call(
        functools.partial(_rglru_kernel, tc=tc),
        grid=(B, n_c),
        in_specs=[
            pl.BlockSpec((None, S, C), blk),
            pl.BlockSpec((None, S, C), blk),
            pl.BlockSpec((CONV_WIDTH, C), par),
            pl.BlockSpec((1, C), par),
            pl.BlockSpec((None, C, 4 * C), lambda b, c: (c, 0, 0)),
            pl.BlockSpec((4, C), par),
            pl.BlockSpec((2, C), par),
        ],
        out_specs=pl.BlockSpec((None, S, C), blk),
        out_shape=jax.ShapeDtypeStruct((B, S, W), BF16),
        scratch_shapes=[pltpu.VMEM((S + 2 * SUBLANES, C), F32), pltpu.VMEM((S, C), F32)],
        compiler_params=_cparams(("arbitrary", "arbitrary")),
        name="rglru",
    )(xr, gate, conv_w, conv_b.reshape(1, W), wg, bg, coef)


ROUTE_ID = N_EXPERTS
GRP_LANE0 = N_EXPERTS


def _outproj_kernel(attn_ref, rnn_ref, x_ref, wa_ref, wr_ref, g_ref, wrt_ref, x2_ref, h2_ref, route_ref):
    mix = (jnp.dot(attn_ref[...], wa_ref[...], preferred_element_type=F32)
           + jnp.dot(rnn_ref[...], wr_ref[...], preferred_element_type=F32))
    x2 = x_ref[...] + mix
    x2_ref[...] = x2
    ms = jnp.mean(x2 * x2, axis=-1, keepdims=True)
    h2 = (x2 * lax.rsqrt(ms + EPS) * g_ref[...]).astype(BF16)
    h2_ref[...] = h2
    logits = jnp.dot(h2, wrt_ref[...], preferred_element_type=F32)
    lane = lax.broadcasted_iota(jnp.int32, logits.shape, 1)
    neg = jnp.float32(-jnp.inf)
    big = jnp.int32(LANES)
    is_g = (lane >= GRP_LANE0) & (lane < GRP_LANE0 + N_GROUPS)
    gl = jnp.where(is_g, logits, neg)
    gmax = jnp.max(gl, axis=-1, keepdims=True)
    gsum = jnp.sum(jnp.where(is_g, jnp.exp(gl - gmax), 0.0), axis=-1, keepdims=True)
    g_top_p = 1.0 / gsum
    g_idx = jnp.min(jnp.where(is_g & (gl == gmax), lane, big), axis=-1, keepdims=True) - GRP_LANE0
    e_lo = g_idx * EXPERTS_PER_GROUP
    in_grp = (lane >= e_lo) & (lane < e_lo + EXPERTS_PER_GROUP)
    el = jnp.where(in_grp, logits, neg)
    e1 = jnp.max(el, axis=-1, keepdims=True)
    i1 = jnp.min(jnp.where(in_grp & (el == e1), lane, big), axis=-1, keepdims=True)
    el2 = jnp.where(lane == i1, neg, el)
    e2 = jnp.max(el2, axis=-1, keepdims=True)
    i2 = jnp.min(jnp.where(in_grp & (lane != i1) & (el2 == e2), lane, big), axis=-1, keepdims=True)
    t = jnp.exp(e2 - e1)
    w1 = g_top_p / (1.0 + t)
    w2 = g_top_p * t / (1.0 + t)
    route = jnp.where(lane == i1, w1, 0.0) + jnp.where(lane == i2, w2, 0.0)
    route = jnp.where(lane == ROUTE_ID, i1.astype(F32), route)
    route = jnp.where(lane == ROUTE_ID + 1, i2.astype(F32), route)
    route = jnp.where(lane == ROUTE_ID + 2, w1, route)
    route = jnp.where(lane == ROUTE_ID + 3, w2, route)
    route_ref[...] = route


def _outproj_router(attn, rnn, x, w_out, g2, w_grp, w_exp, tm):
    N, D = x.shape
    w_out = w_out.astype(BF16)
    wrt = jnp.zeros((D, LANES), F32).at[:, :N_EXPERTS].set(w_exp).at[:, GRP_LANE0:GRP_LANE0 + N_GROUPS].set(w_grp)
    row = lambda i: (i, 0)
    const = lambda i: (0, 0)
    return pl.pallas_call(
        _outproj_kernel,
        grid=(N // tm,),
        in_specs=[
            pl.BlockSpec((tm, ATTN_WIDTH), row),
            pl.BlockSpec((tm, LRU_WIDTH), row),
            pl.BlockSpec((tm, D), row),
            pl.BlockSpec((ATTN_WIDTH, D), const),
            pl.BlockSpec((LRU_WIDTH, D), const),
            pl.BlockSpec((1, D), const),
            pl.BlockSpec((D, LANES), const),
        ],
        out_specs=[pl.BlockSpec((tm, D), row), pl.BlockSpec((tm, D), row), pl.BlockSpec((tm, LANES), row)],
        out_shape=[jax.ShapeDtypeStruct((N, D), F32), jax.ShapeDtypeStruct((N, D), BF16),
                   jax.ShapeDtypeStruct((N, LANES), F32)],
        compiler_params=_cparams(("arbitrary",)),
        name="outproj_router",
    )(attn, rnn, x, w_out[:ATTN_WIDTH], w_out[ATTN_WIDTH:], g2.reshape(1, D), wrt.astype(BF16))


def _moe_dense_kernel(h_ref, route_ref, x2_ref, wg_ref, wu_ref, wd_ref, fg_ref, o_ref, acc_ref):
    e = pl.program_id(1)

    @pl.when(e == 0)
    def _():
        acc_ref[...] = x2_ref[...]

    h = h_ref[...]
    gt = jnp.dot(h, wg_ref[...], preferred_element_type=F32)
    up = jnp.dot(h, wu_ref[...], preferred_element_type=F32)
    hid = (gt * jax.nn.sigmoid(gt)) * up
    lane = lax.broadcasted_iota(jnp.int32, route_ref.shape, 1)
    cw = jnp.sum(jnp.where(lane == e, route_ref[...], 0.0), axis=-1, keepdims=True)
    acc_ref[...] += cw * jnp.dot(hid.astype(BF16), wd_ref[...], preferred_element_type=F32)

    @pl.when(e == pl.num_programs(1) - 1)
    def _():
        y = acc_ref[...]
        ms = jnp.mean(y * y, axis=-1, keepdims=True)
        o_ref[...] = y * lax.rsqrt(ms + EPS) * fg_ref[...]


def _moe_dense(h2, route, x2, w_gate, w_up, w_down, final_g, tm):
    N, D = h2.shape
    row = lambda i, e: (i, 0)
    const = lambda i, e: (0, 0)
    wmap = lambda i, e: (e, 0, 0)
    return pl.pallas_call(
        _moe_dense_kernel,
        grid=(N // tm, N_EXPERTS),
        in_specs=[
            pl.BlockSpec((tm, D), row),
            pl.BlockSpec((tm, LANES), row),
            pl.BlockSpec((tm, D), row),
            pl.BlockSpec((None, D, D_EXPERT), wmap),
            pl.BlockSpec((None, D, D_EXPERT), wmap),
            pl.BlockSpec((None, D_EXPERT, D), wmap),
            pl.BlockSpec((1, D), const),
        ],
        out_specs=pl.BlockSpec((tm, D), row),
        out_shape=jax.ShapeDtypeStruct((N, D), F32),
        scratch_shapes=[pltpu.VMEM((tm, D), F32)],
        compiler_params=_cparams(("arbitrary", "arbitrary")),
        name="moe_dense",
    )(h2, route, x2, w_gate.astype(BF16), w_up.astype(BF16), w_down.astype(BF16), final_g.reshape(1, D))


def _tiles(B, S):
    return dict(
        tm_in=min(512, S),
        tq=min(256, S),
        tc=min(64, S),
        tm_out=min(512, S),
        tm_moe=min(1024, S),
    )


def kernel(x, norm1_g, w_in, lambda_q1, lambda_k1, lambda_q2, lambda_k2, subln_g, conv_w, conv_b,
           lru_w_r, lru_b_r, lru_w_i, lru_b_i, lru_lambda, w_out, norm2_g, w_grp, w_exp, w_gate, w_up,
           w_down, final_g):
    B, S, D = x.shape
    t = _tiles(B, S)
    l = 0
    q, k, v, xr, gate = _inproj(x, norm1_g[l], w_in[l], t["tm_in"])
    lam_vecs = jnp.zeros((SUBLANES, LANES), F32).at[0:4, :HEAD_DIM].set(
        jnp.stack([lambda_q1[l], lambda_k1[l], lambda_q2[l], lambda_k2[l]]))
    attn = _attention(q, k, v, lam_vecs, subln_g[l], t["tq"])
    rnn = _rglru(xr, gate, conv_w[l], conv_b[l], lru_w_r[l], lru_b_r[l], lru_w_i[l], lru_b_i[l],
                 lru_lambda[l], t["tc"])
    N = B * S
    x2, h2, route = _outproj_router(attn.reshape(N, ATTN_WIDTH), rnn.reshape(N, LRU_WIDTH),
                                    x.reshape(N, D), w_out[l], norm2_g[l], w_grp[l], w_exp[l], t["tm_out"])
    out = _moe_dense(h2, route, x2, w_gate[l], w_up[l], w_down[l], final_g, t["tm_moe"])
    return out.reshape(B, S, D)
```

```python
import functools
import math

import jax
import jax.numpy as jnp
from jax import lax
from jax.experimental import pallas as pl
from jax.experimental.pallas import tpu as pltpu

F32 = jnp.float32
BF16 = jnp.bfloat16

D_MODEL = 1024
ATTN_HEADS = 4
HEAD_DIM = 64
V_DIM = 2 * HEAD_DIM
ATTN_WIDTH = ATTN_HEADS * V_DIM
ROPE_THETA = 500000.0
ROPE_DIM = HEAD_DIM // 4
LRU_WIDTH = D_MODEL - ATTN_WIDTH
LRU_BLOCK_DIM = 64
CONV_WIDTH = 4
LRU_C = 8.0
N_GROUPS = 4
EXPERTS_PER_GROUP = 8
N_EXPERTS = N_GROUPS * EXPERTS_PER_GROUP
D_EXPERT = 256
EPS = 1e-6
LAMBDA_INIT = 0.8 - 0.6 * math.exp(0.0)

LANES = 128
SUBLANES = 8
MXU_DIM = 256
VMEM_LIMIT = 56 * 1024 * 1024


def _cparams(semantics):
    return pltpu.CompilerParams(dimension_semantics=semantics, vmem_limit_bytes=VMEM_LIMIT)


def _rope_tables(seq_len):
    pos = jnp.arange(seq_len, dtype=F32)
    inv_freq = ROPE_THETA ** (-jnp.arange(0, ROPE_DIM, 2, dtype=F32) / ROPE_DIM)
    ang = pos[:, None] * inv_freq[None, :]
    cos, sin = jnp.cos(ang), jnp.sin(ang)
    half = ROPE_DIM // 2
    j = jnp.arange(LANES) % HEAD_DIM
    f = j % half
    c = jnp.where(j[None, :] < ROPE_DIM, cos[:, f], 1.0)
    s_lo = jnp.where(j[None, :] < half, -sin[:, f], 0.0)
    s_hi = jnp.where((j[None, :] >= half) & (j[None, :] < ROPE_DIM), sin[:, f], 0.0)
    return c, s_lo, s_hi


def _inproj_kernel(x_ref, g_ref, w_ref, c_ref, slo_ref, shi_ref,
                   q_ref, k_ref, v_ref, xr_ref, gate_ref):
    x = x_ref[...]
    ms = jnp.mean(x * x, axis=-1, keepdims=True)
    h = (x * lax.rsqrt(ms + EPS) * g_ref[...]).astype(BF16)
    c, slo, shi = c_ref[...], slo_ref[...], shi_ref[...]

    def rope(p):
        return p * c + pltpu.roll(p, LANES - ROPE_DIM // 2, 1) * slo + pltpu.roll(p, ROPE_DIM // 2, 1) * shi

    q_scale = HEAD_DIM ** -0.5
    for hd in range(ATTN_HEADS):
        lo = hd * V_DIM
        pq = jnp.dot(h, w_ref[:, lo:lo + V_DIM], preferred_element_type=F32)
        q_ref[:, lo:lo + V_DIM] = (rope(pq) * q_scale).astype(BF16)
        pk = jnp.dot(h, w_ref[:, ATTN_WIDTH + lo:ATTN_WIDTH + lo + V_DIM], preferred_element_type=F32)
        k_ref[:, lo:lo + V_DIM] = rope(pk).astype(BF16)
    off = 2 * ATTN_WIDTH
    v_ref[...] = jnp.dot(h, w_ref[:, off:off + ATTN_WIDTH], preferred_element_type=F32).astype(BF16)
    off += ATTN_WIDTH
    xr_ref[...] = jnp.dot(h, w_ref[:, off:off + LRU_WIDTH], preferred_element_type=F32).astype(BF16)
    off += LRU_WIDTH
    gate_ref[...] = jnp.dot(h, w_ref[:, off:off + LRU_WIDTH], preferred_element_type=F32).astype(BF16)


def _inproj(x, g, w_in, tm):
    B, S, D = x.shape
    c, slo, shi = _rope_tables(S)
    row = lambda si, b: (b, si, 0)
    tab = lambda si, b: (si, 0)
    const = lambda si, b: (0, 0)
    out = jax.ShapeDtypeStruct((B, S, ATTN_WIDTH), BF16)
    return pl.pallas_call(
        _inproj_kernel,
        grid=(S // tm, B),
        in_specs=[
            pl.BlockSpec((None, tm, D), row),
            pl.BlockSpec((1, D), const),
            pl.BlockSpec(w_in.shape, const),
            pl.BlockSpec((tm, LANES), tab),
            pl.BlockSpec((tm, LANES), tab),
            pl.BlockSpec((tm, LANES), tab),
        ],
        out_specs=[pl.BlockSpec((None, tm, ATTN_WIDTH), row)] * 5,
        out_shape=[out] * 5,
        compiler_params=_cparams(("arbitrary", "arbitrary")),
        name="inproj",
    )(x, g.reshape(1, D), w_in.astype(BF16), c, slo, shi)


def _attn_kernel(lam_ref, q_ref, k_ref, v_ref, sg_ref, o_ref):
    lv = lam_ref[...]
    lam = (jnp.exp(jnp.sum(lv[0:1] * lv[1:2], axis=-1, keepdims=True))
           - jnp.exp(jnp.sum(lv[2:3] * lv[3:4], axis=-1, keepdims=True)) + LAMBDA_INIT)
    q = q_ref[...]
    k = k_ref[...]
    v = v_ref[...]
    lane = lax.broadcasted_iota(jnp.int32, q.shape, 1)
    zero = jnp.zeros_like(q)

    def softmax_v(qm):
        s = lax.dot_general(qm, k, (((1,), (1,)), ((), ())), preferred_element_type=F32)
        m = jnp.max(s, axis=-1, keepdims=True)
        e = jnp.exp(s - m)
        l = jnp.sum(e, axis=-1, keepdims=True)
        return jnp.dot(e.astype(BF16), v, preferred_element_type=F32) / l

    o = softmax_v(jnp.where(lane < HEAD_DIM, q, zero)) - lam * softmax_v(jnp.where(lane >= HEAD_DIM, q, zero))
    ms = jnp.mean(o * o, axis=-1, keepdims=True)
    o_ref[...] = (o * lax.rsqrt(ms + EPS) * sg_ref[...] * (1.0 - LAMBDA_INIT)).astype(o_ref.dtype)


def _attention(q, k, v, lam_vecs, subln_g, tq):
    B, S, _ = q.shape
    qmap = lambda b, h, i: (b, i, h)
    kmap = lambda b, h, i: (b, 0, h)
    const = lambda b, h, i: (0, 0)
    return pl.pallas_call(
        _attn_kernel,
        grid=(B, ATTN_HEADS, S // tq),
        in_specs=[
            pl.BlockSpec((SUBLANES, LANES), const),
            pl.BlockSpec((None, tq, V_DIM), qmap),
            pl.BlockSpec((None, S, V_DIM), kmap),
            pl.BlockSpec((None, S, V_DIM), kmap),
            pl.BlockSpec((1, V_DIM), const),
        ],
        out_specs=pl.BlockSpec((None, tq, V_DIM), qmap),
        out_shape=jax.ShapeDtypeStruct((B, S, ATTN_WIDTH), BF16),
        compiler_params=_cparams(("arbitrary", "arbitrary", "arbitrary")),
        name="diffattn",
    )(lam_vecs, q, k, v, subln_g.reshape(1, V_DIM))


def _shift_rows(x, d, fill, up):
    n = x.shape[0]
    row = lax.broadcasted_iota(jnp.int32, x.shape, 0)
    if up:
        return jnp.where(row < n - d, pltpu.roll(x, n - d, 0), fill)
    return jnp.where(row >= d, pltpu.roll(x, d, 0), fill)


def _chunk_scan(a, u, reverse):
    n = a.shape[0]
    d = 1
    while d < n:
        a_s = _shift_rows(a, d, 1.0, reverse)
        u_s = _shift_rows(u, d, 0.0, reverse)
        u = a * u_s + u
        a = a * a_s
        d *= 2
    return a, u


def _rglru_kernel(xr_ref, gate_ref, cw_ref, cb_ref, wg_ref, bg_ref, coef_ref, o_ref, xp_ref, hf_ref, *, tc):
    S, C = xr_ref.shape
    pad = SUBLANES
    n_chunks = S // tc
    xp_ref[0:pad, :] = jnp.zeros((pad, C), F32)
    xp_ref[pad + S:pad + S + pad, :] = jnp.zeros((pad, C), F32)
    xp_ref[pad:pad + S, :] = xr_ref[...].astype(F32)
    cw = cw_ref[...]
    cb = cb_ref[...]
    wg = wg_ref[...]
    bg = bg_ref[...]
    coef = coef_ref[...]

    def gates(r0, direction):
        w = xp_ref[pl.ds(r0, tc + 2 * pad), :]
        n = tc + 2 * pad
        xc = (w * cw[2:3] + pltpu.roll(w, 2, 0) * cw[0:1] + pltpu.roll(w, 1, 0) * cw[1:2]
              + pltpu.roll(w, n - 1, 0) * cw[3:4])[pad:pad + tc] + cb
        lin = jnp.dot(xc.astype(BF16), wg[:, 2 * C * direction:2 * C * (direction + 1)],
                      preferred_element_type=F32)
        r = jax.nn.sigmoid(lin[:, :C] + bg[2 * direction:2 * direction + 1])
        i = jax.nn.sigmoid(lin[:, C:] + bg[2 * direction + 1:2 * direction + 2])
        log_a = coef[direction:direction + 1] * r
        a = jnp.exp(log_a)
        th = jnp.tanh(log_a)
        u = jnp.sqrt(-2.0 * th / (1.0 - th)) * (i * xc)
        return a, u

    def fwd(ci, h_prev):
        r0 = pl.multiple_of(ci * tc, tc)
        a, u = gates(r0, 0)
        a_c, h0 = _chunk_scan(a, u, False)
        h = a_c * h_prev + h0
        hf_ref[pl.ds(r0, tc), :] = h
        return jnp.broadcast_to(h[tc - 1:tc], h_prev.shape)

    lax.fori_loop(0, n_chunks, fwd, jnp.zeros((tc, C), F32))

    def bwd(cj, h_next):
        r0 = pl.multiple_of((n_chunks - 1 - cj) * tc, tc)
        a, u = gates(r0, 1)
        a_c, h0 = _chunk_scan(a, u, True)
        h = a_c * h_next + h0
        g = gate_ref[pl.ds(r0, tc), :].astype(F32)
        o_ref[pl.ds(r0, tc), :] = ((hf_ref[pl.ds(r0, tc), :] + h) * jax.nn.gelu(g)).astype(o_ref.dtype)
        return jnp.broadcast_to(h[0:1], h_next.shape)

    lax.fori_loop(0, n_chunks, bwd, jnp.zeros((tc, C), F32))


def _block_diag(w):
    nb, d, _ = w.shape
    eye = jnp.eye(nb, dtype=w.dtype)
    return (eye[:, None, :, None] * w[:, :, None, :]).reshape(nb * d, nb * d)


def _rglru(xr, gate, conv_w, conv_b, w_r, b_r, w_i, b_i, lru_lambda, tc):
    B, S, W = xr.shape
    C = MXU_DIM
    n_c = W // C
    mats = [_block_diag(w_r[0]), _block_diag(w_i[0]), _block_diag(w_r[1]), _block_diag(w_i[1])]
    wg = jnp.stack([jnp.concatenate([m[c * C:(c + 1) * C, c * C:(c + 1) * C] for m in mats], axis=1)
                    for c in range(n_c)]).astype(BF16)
    bg = jnp.stack([b_r[0], b_i[0], b_r[1], b_i[1]])
    coef = -LRU_C * jax.nn.softplus(-lru_lambda.astype(F32))
    blk = lambda b, c: (b, 0, c)
    par = lambda b, c: (0, c)
    return pl.pallas_call(
        functools.partial(_rglru_kernel, tc=tc),
        grid=(B, n_c),
        in_specs=[
            pl.BlockSpec((None, S, C), blk),
            pl.BlockSpec((None, S, C), blk),
            pl.BlockSpec((CONV_WIDTH, C), par),
            pl.BlockSpec((1, C), par),
            pl.BlockSpec((None, C, 4 * C), lambda b, c: (c, 0, 0)),
            pl.BlockSpec((4, C), par),
            pl.BlockSpec((2, C), par),
        ],
        out_specs=pl.BlockSpec((None, S, C), blk),
        out_shape=jax.ShapeDtypeStruct((B, S, W), BF16),
        scratch_shapes=[pltpu.VMEM((S + 2 * SUBLANES, C), F32), pltpu.VMEM((S, C), F32)],
        compiler_params=_cparams(("arbitrary", "arbitrary")),
        name="rglru",
    )(xr, gate, conv_w, conv_b.reshape(1, W), wg, bg, coef)


GRP_LANE0 = N_EXPERTS
R_ID, R_W, R_RANK = 0, 2, 4


def _outproj_kernel(attn_ref, rnn_ref, x_ref, wa_ref, wr_ref, g_ref, wrt_ref, tri_ref,
                    x2_ref, h2_ref, route_ref, cnt_ref, run_ref):
    @pl.when(pl.program_id(0) == 0)
    def _():
        run_ref[...] = jnp.zeros_like(run_ref)

    mix = (jnp.dot(attn_ref[...], wa_ref[...], preferred_element_type=F32)
           + jnp.dot(rnn_ref[...], wr_ref[...], preferred_element_type=F32))
    x2 = x_ref[...] + mix
    x2_ref[...] = x2
    ms = jnp.mean(x2 * x2, axis=-1, keepdims=True)
    h2 = x2 * lax.rsqrt(ms + EPS) * g_ref[...]
    h2_ref[...] = h2
    logits = jnp.dot(h2.astype(BF16), wrt_ref[...], preferred_element_type=F32)
    lane = lax.broadcasted_iota(jnp.int32, logits.shape, 1)
    neg = jnp.float32(-jnp.inf)
    big = jnp.int32(LANES)
    is_g = (lane >= GRP_LANE0) & (lane < GRP_LANE0 + N_GROUPS)
    gl = jnp.where(is_g, logits, neg)
    gmax = jnp.max(gl, axis=-1, keepdims=True)
    gsum = jnp.sum(jnp.where(is_g, jnp.exp(gl - gmax), 0.0), axis=-1, keepdims=True)
    g_top_p = 1.0 / gsum
    g_idx = jnp.min(jnp.where(is_g & (gl == gmax), lane, big), axis=-1, keepdims=True) - GRP_LANE0
    e_lo = g_idx * EXPERTS_PER_GROUP
    in_grp = (lane >= e_lo) & (lane < e_lo + EXPERTS_PER_GROUP)
    el = jnp.where(in_grp, logits, neg)
    e1 = jnp.max(el, axis=-1, keepdims=True)
    i1 = jnp.min(jnp.where(in_grp & (el == e1), lane, big), axis=-1, keepdims=True)
    el2 = jnp.where(lane == i1, neg, el)
    e2 = jnp.max(el2, axis=-1, keepdims=True)
    i2 = jnp.min(jnp.where(in_grp & (lane != i1) & (el2 == e2), lane, big), axis=-1, keepdims=True)
    t = jnp.exp(e2 - e1)
    w1 = g_top_p / (1.0 + t)
    w2 = g_top_p * t / (1.0 + t)
    oh1 = lane == i1
    oh2 = lane == i2
    oh = jnp.where(oh1 | oh2, 1.0, 0.0)
    before = jnp.dot(tri_ref[...], oh.astype(BF16), preferred_element_type=F32) + run_ref[...]
    rank1 = jnp.sum(jnp.where(oh1, before, 0.0), axis=-1, keepdims=True)
    rank2 = jnp.sum(jnp.where(oh2, before, 0.0), axis=-1, keepdims=True)
    run = run_ref[...] + jnp.sum(oh, axis=0, keepdims=True)
    run_ref[...] = run
    cnt_ref[...] = jnp.broadcast_to(run, cnt_ref.shape)
    route = jnp.where(lane == R_ID, i1.astype(F32), 0.0)
    route = jnp.where(lane == R_ID + 1, i2.astype(F32), route)
    route = jnp.where(lane == R_W, w1, route)
    route = jnp.where(lane == R_W + 1, w2, route)
    route = jnp.where(lane == R_RANK, rank1, route)
    route = jnp.where(lane == R_RANK + 1, rank2, route)
    route_ref[...] = route


def _outproj_router(attn, rnn, x, w_out, g2, w_grp, w_exp, tm):
    N, D = x.shape
    w_out = w_out.astype(BF16)
    wrt = jnp.zeros((D, LANES), F32).at[:, :N_EXPERTS].set(w_exp).at[:, GRP_LANE0:GRP_LANE0 + N_GROUPS].set(w_grp)
    tri = jnp.tril(jnp.ones((tm, tm), BF16), -1)
    row = lambda i: (i, 0)
    const = lambda i: (0, 0)
    return pl.pallas_call(
        _outproj_kernel,
        grid=(N // tm,),
        in_specs=[
            pl.BlockSpec((tm, ATTN_WIDTH), row),
            pl.BlockSpec((tm, LRU_WIDTH), row),
            pl.BlockSpec((tm, D), row),
            pl.BlockSpec((ATTN_WIDTH, D), const),
            pl.BlockSpec((LRU_WIDTH, D), const),
            pl.BlockSpec((1, D), const),
            pl.BlockSpec((D, LANES), const),
            pl.BlockSpec((tm, tm), const),
        ],
        out_specs=[pl.BlockSpec((tm, D), row), pl.BlockSpec((tm, D), row), pl.BlockSpec((tm, LANES), row),
                   pl.BlockSpec((SUBLANES, LANES), const)],
        out_shape=[jax.ShapeDtypeStruct((N, D), F32), jax.ShapeDtypeStruct((N, D), F32),
                   jax.ShapeDtypeStruct((N, LANES), F32), jax.ShapeDtypeStruct((SUBLANES, LANES), F32)],
        scratch_shapes=[pltpu.VMEM((1, LANES), F32)],
        compiler_params=_cparams(("arbitrary",)),
        name="outproj_router",
    )(attn, rnn, x, w_out[:ATTN_WIDTH], w_out[ATTN_WIDTH:], g2.reshape(1, D), wrt.astype(BF16), tri)


def _dispatch_plan(route, counts, tg):
    N = route.shape[0]
    ids = route[:, R_ID:R_ID + 2].astype(jnp.int32)
    rank = route[:, R_RANK:R_RANK + 2].astype(jnp.int32)
    cnt = counts[0, :N_EXPERTS].astype(jnp.int32)
    tiles_e = (cnt + tg - 1) // tg
    tile_end = jnp.cumsum(tiles_e)
    row_off = (tile_end - tiles_e) * tg
    pos = (row_off[ids] + rank).reshape(2 * N)
    n_tiles = (2 * N) // tg + N_EXPERTS
    n_act = tile_end[-1]
    tile_expert = jnp.minimum(jnp.searchsorted(tile_end, jnp.arange(n_tiles, dtype=jnp.int32), side="right"),
                              N_EXPERTS - 1).astype(jnp.int32)
    tile_expert = jnp.where(jnp.arange(n_tiles) < n_act, tile_expert, tile_expert[jnp.maximum(n_act - 1, 0)])
    pad_start = row_off + cnt
    pad_n = tiles_e * tg - cnt
    return pos, tile_expert, n_act.reshape(1).astype(jnp.int32), pad_start, pad_n, n_tiles


def _row_copy(src, src_row, dst, dst_row, sem):
    return pltpu.make_async_copy(src.at[pl.ds(src_row, 1)], dst.at[pl.ds(dst_row, 1)], sem)


def _scatter_kernel(pad_start_ref, pad_n_ref, nact_ref, pos_ref, h_ref, xs_ref, zero_ref, sem, *, tg):
    tm = h_ref.shape[0]
    half = tg // 2

    @pl.when(pl.program_id(0) == 0)
    def _():
        zero_ref[...] = jnp.zeros_like(zero_ref)

        def fill_tile(j, c):
            base = pl.multiple_of(j * tg, tg)
            cps = [pltpu.make_async_copy(zero_ref, xs_ref.at[pl.ds(base + h * half, half)], sem) for h in range(2)]
            for cp in cps:
                cp.start()
            for cp in cps:
                cp.wait()
            return c

        lax.fori_loop(nact_ref[0], xs_ref.shape[0] // tg, fill_tile, 0)

    @pl.when(pl.program_id(0) == 0)
    def _():
        for e in range(N_EXPERTS):
            n = pad_n_ref[e]
            off = pad_start_ref[e]
            b = 1
            while b < tg:
                hit = (n & b) != 0

                @pl.when(hit)
                def _(off=off, b=b):
                    if b < SUBLANES:
                        cps = [_row_copy(zero_ref, 0, xs_ref, off + i, sem) for i in range(b)]
                    else:
                        cps = [pltpu.make_async_copy(zero_ref.at[pl.ds(0, b)],
                                                     xs_ref.at[pl.ds(pl.multiple_of(off, SUBLANES), b)], sem)]
                    for cp in cps:
                        cp.start()
                    for cp in cps:
                        cp.wait()

                off = off + jnp.where(hit, b, 0)
                b *= 2

    def issue(r, c):
        _row_copy(h_ref, r, xs_ref, pos_ref[2 * r], sem).start()
        _row_copy(h_ref, r, xs_ref, pos_ref[2 * r + 1], sem).start()
        return c

    lax.fori_loop(0, tm, issue, 0, unroll=8)

    def drain(r, c):
        _row_copy(h_ref, 0, xs_ref, 0, sem).wait()
        _row_copy(h_ref, 0, xs_ref, 0, sem).wait()
        return c

    lax.fori_loop(0, tm, drain, 0, unroll=8)


def _scatter_rows(h2, pos, pad_start, pad_n, n_act, n_rows, tm, tg):
    N, D = h2.shape
    return pl.pallas_call(
        functools.partial(_scatter_kernel, tg=tg),
        grid_spec=pltpu.PrefetchScalarGridSpec(
            num_scalar_prefetch=3,
            grid=(N // tm,),
            in_specs=[
                pl.BlockSpec((2 * tm,), lambda i, ps, pn, na: (i,), memory_space=pltpu.SMEM),
                pl.BlockSpec((tm, D), lambda i, ps, pn, na: (i, 0)),
            ],
            out_specs=pl.BlockSpec(memory_space=pl.ANY),
            scratch_shapes=[pltpu.VMEM((tg // 2, D), F32), pltpu.SemaphoreType.DMA],
        ),
        out_shape=jax.ShapeDtypeStruct((n_rows, D), F32),
        compiler_params=_cparams(("arbitrary",)),
        name="moe_scatter",
    )(pad_start, pad_n, n_act, pos, h2)


def _experts_kernel(te_ref, nact_ref, x_ref, wg_ref, wu_ref, wd_ref, y_ref):
    active = pl.program_id(0) < nact_ref[0]

    @pl.when(active)
    def _():
        x = x_ref[...].astype(BF16)
        gt = jnp.dot(x, wg_ref[...], preferred_element_type=F32)
        up = jnp.dot(x, wu_ref[...], preferred_element_type=F32)
        hid = (gt * jax.nn.sigmoid(gt)) * up
        y_ref[...] = jnp.dot(hid.astype(BF16), wd_ref[...], preferred_element_type=F32)

    @pl.when(jnp.logical_not(active))
    def _():
        y_ref[...] = jnp.zeros_like(y_ref)


def _experts(xs, tile_expert, n_act, w_gate, w_up, w_down, tg):
    R, D = xs.shape
    rows = lambda j, te, na: (j, 0)
    wmap = lambda j, te, na: (te[j], 0, 0)
    return pl.pallas_call(
        _experts_kernel,
        grid_spec=pltpu.PrefetchScalarGridSpec(
            num_scalar_prefetch=2,
            grid=(R // tg,),
            in_specs=[
                pl.BlockSpec((tg, D), rows),
                pl.BlockSpec((None, D, D_EXPERT), wmap),
                pl.BlockSpec((None, D, D_EXPERT), wmap),
                pl.BlockSpec((None, D_EXPERT, D), wmap),
            ],
            out_specs=pl.BlockSpec((tg, D), rows),
        ),
        out_shape=jax.ShapeDtypeStruct((R, D), F32),
        compiler_params=_cparams(("arbitrary",)),
        name="moe_experts",
    )(tile_expert, n_act, xs, w_gate.astype(BF16), w_up.astype(BF16), w_down.astype(BF16))


def _combine_kernel(pos_ref, route_ref, x2_ref, fg_ref, ys_ref, o_ref, buf_ref, sem):
    tm = x2_ref.shape[0]

    def issue(r, c):
        _row_copy(ys_ref, pos_ref[2 * r], buf_ref.at[0], r, sem).start()
        _row_copy(ys_ref, pos_ref[2 * r + 1], buf_ref.at[1], r, sem).start()
        return c

    lax.fori_loop(0, tm, issue, 0, unroll=8)

    def drain(r, c):
        _row_copy(ys_ref, 0, buf_ref.at[0], 0, sem).wait()
        _row_copy(ys_ref, 0, buf_ref.at[0], 0, sem).wait()
        return c

    lax.fori_loop(0, tm, drain, 0, unroll=8)
    route = route_ref[...]
    lane = lax.broadcasted_iota(jnp.int32, route.shape, 1)
    w1 = jnp.sum(jnp.where(lane == R_W, route, 0.0), axis=-1, keepdims=True)
    w2 = jnp.sum(jnp.where(lane == R_W + 1, route, 0.0), axis=-1, keepdims=True)
    y = x2_ref[...] + w1 * buf_ref[0] + w2 * buf_ref[1]
    ms = jnp.mean(y * y, axis=-1, keepdims=True)
    o_ref[...] = y * lax.rsqrt(ms + EPS) * fg_ref[...]


def _combine(ys, pos, route, x2, final_g, tm):
    N, D = x2.shape
    row = lambda i: (i, 0)
    return pl.pallas_call(
        _combine_kernel,
        grid=(N // tm,),
        in_specs=[
            pl.BlockSpec((2 * tm,), lambda i: (i,), memory_space=pltpu.SMEM),
            pl.BlockSpec((tm, LANES), row),
            pl.BlockSpec((tm, D), row),
            pl.BlockSpec((1, D), lambda i: (0, 0)),
            pl.BlockSpec(memory_space=pl.ANY),
        ],
        out_specs=pl.BlockSpec((tm, D), row),
        out_shape=jax.ShapeDtypeStruct((N, D), F32),
        scratch_shapes=[pltpu.VMEM((2, tm, D), F32), pltpu.SemaphoreType.DMA],
        compiler_params=_cparams(("arbitrary",)),
        name="moe_combine",
    )(pos, route, x2, final_g.reshape(1, D), ys)


def _tiles(B, S):
    return dict(
        tm_in=min(512, S),
        tq=min(256, S),
        tc=min(64, S),
        tm_out=min(512, S),
        tm_row=min(256, S),
        tg=min(512, S),
    )


def kernel(x, norm1_g, w_in, lambda_q1, lambda_k1, lambda_q2, lambda_k2, subln_g, conv_w, conv_b,
           lru_w_r, lru_b_r, lru_w_i, lru_b_i, lru_lambda, w_out, norm2_g, w_grp, w_exp, w_gate, w_up,
           w_down, final_g):
    B, S, D = x.shape
    t = _tiles(B, S)
    l = 0
    q, k, v, xr, gate = _inproj(x, norm1_g[l], w_in[l], t["tm_in"])
    lam_vecs = jnp.zeros((SUBLANES, LANES), F32).at[0:4, :HEAD_DIM].set(
        jnp.stack([lambda_q1[l], lambda_k1[l], lambda_q2[l], lambda_k2[l]]))
    attn = _attention(q, k, v, lam_vecs, subln_g[l], t["tq"])
    rnn = _rglru(xr, gate, conv_w[l], conv_b[l], lru_w_r[l], lru_b_r[l], lru_w_i[l], lru_b_i[l],
                 lru_lambda[l], t["tc"])
    N = B * S
    x2, h2, route, counts = _outproj_router(attn.reshape(N, ATTN_WIDTH), rnn.reshape(N, LRU_WIDTH),
                                            x.reshape(N, D), w_out[l], norm2_g[l], w_grp[l], w_exp[l],
                                            t["tm_out"])
    tg = t["tg"]
    pos, tile_expert, n_act, pad_start, pad_n, n_tiles = _dispatch_plan(route, counts, tg)
    xs = _scatter_rows(h2, pos, pad_start, pad_n, n_act, n_tiles * tg, t["tm_row"], tg)
    ys = _experts(xs, tile_expert, n_act, w_gate[l], w_up[l], w_down[l], tg)
    out = _combine(ys, pos, route, x2, final_g, t["tm_row"])
    return out.reshape(B, S, D)
```

```python
import functools
import math

import jax
import jax.numpy as jnp
from jax import lax
from jax.experimental import pallas as pl
from jax.experimental.pallas import tpu as pltpu

F32 = jnp.float32
BF16 = jnp.bfloat16

D_MODEL = 1024
ATTN_HEADS = 4
HEAD_DIM = 64
V_DIM = 2 * HEAD_DIM
ATTN_WIDTH = ATTN_HEADS * V_DIM
ROPE_THETA = 500000.0
ROPE_DIM = HEAD_DIM // 4
LRU_WIDTH = D_MODEL - ATTN_WIDTH
LRU_BLOCK_DIM = 64
CONV_WIDTH = 4
LRU_C = 8.0
N_GROUPS = 4
EXPERTS_PER_GROUP = 8
N_EXPERTS = N_GROUPS * EXPERTS_PER_GROUP
D_EXPERT = 256
EPS = 1e-6
LAMBDA_INIT = 0.8 - 0.6 * math.exp(0.0)

LANES = 128
SUBLANES = 8
MXU_DIM = 256
BF16_SUBLANES = 16
VT_ROWS = V_DIM + BF16_SUBLANES
VMEM_LIMIT = 56 * 1024 * 1024


def _cparams(semantics, flags=None):
    return pltpu.CompilerParams(dimension_semantics=semantics, vmem_limit_bytes=VMEM_LIMIT, flags=flags)


def _rope_tables(seq_len):
    pos = jnp.arange(seq_len, dtype=F32)
    inv_freq = ROPE_THETA ** (-jnp.arange(0, ROPE_DIM, 2, dtype=F32) / ROPE_DIM)
    ang = pos[:, None] * inv_freq[None, :]
    cos, sin = jnp.cos(ang), jnp.sin(ang)
    half = ROPE_DIM // 2
    j = jnp.arange(LANES) % HEAD_DIM
    f = j % half
    c = jnp.where(j[None, :] < ROPE_DIM, cos[:, f], 1.0)
    s_lo = jnp.where(j[None, :] < half, -sin[:, f], 0.0)
    s_hi = jnp.where((j[None, :] >= half) & (j[None, :] < ROPE_DIM), sin[:, f], 0.0)
    return c, s_lo, s_hi


def _inproj_kernel(x_ref, g_ref, w_ref, wvt_ref, c_ref, slo_ref, shi_ref,
                   q_ref, k_ref, vt_ref, xr_ref, gate_ref):
    x = x_ref[...]
    ms = jnp.mean(x * x, axis=-1, keepdims=True)
    h = (x * lax.rsqrt(ms + EPS) * g_ref[...]).astype(BF16)
    c, slo, shi = c_ref[...], slo_ref[...], shi_ref[...]

    def rope(p):
        return p * c + pltpu.roll(p, LANES - ROPE_DIM // 2, 1) * slo + pltpu.roll(p, ROPE_DIM // 2, 1) * shi

    q_scale = HEAD_DIM ** -0.5 * math.log2(math.e)
    for hd in range(ATTN_HEADS):
        lo = hd * V_DIM
        pq = jnp.dot(h, w_ref[:, lo:lo + V_DIM], preferred_element_type=F32)
        q_ref[:, lo:lo + V_DIM] = (rope(pq) * q_scale).astype(BF16)
        pk = jnp.dot(h, w_ref[:, ATTN_WIDTH + lo:ATTN_WIDTH + lo + V_DIM], preferred_element_type=F32)
        k_ref[:, lo:lo + V_DIM] = rope(pk).astype(BF16)
    vt = lax.dot_general(wvt_ref[...], h, (((1,), (1,)), ((), ())), preferred_element_type=F32).astype(BF16)
    for hd in range(ATTN_HEADS):
        vt_ref[hd * VT_ROWS:hd * VT_ROWS + V_DIM, :] = vt[hd * V_DIM:(hd + 1) * V_DIM]
        vt_ref[hd * VT_ROWS + V_DIM:(hd + 1) * VT_ROWS, :] = jnp.ones((VT_ROWS - V_DIM, vt.shape[1]), BF16)
    off = 3 * ATTN_WIDTH
    xr_ref[...] = jnp.dot(h, w_ref[:, off:off + LRU_WIDTH], preferred_element_type=F32).astype(BF16)
    off += LRU_WIDTH
    gate_ref[...] = jnp.dot(h, w_ref[:, off:off + LRU_WIDTH], preferred_element_type=F32).astype(BF16)


def _inproj(x, g, w_in, tm):
    B, S, D = x.shape
    c, slo, shi = _rope_tables(S)
    row = lambda si, b: (b, si, 0)
    tab = lambda si, b: (si, 0)
    const = lambda si, b: (0, 0)
    out = jax.ShapeDtypeStruct((B, S, ATTN_WIDTH), BF16)
    w_in = w_in.astype(BF16)
    wvt = w_in[:, 2 * ATTN_WIDTH:3 * ATTN_WIDTH].T
    return pl.pallas_call(
        _inproj_kernel,
        grid=(S // tm, B),
        in_specs=[
            pl.BlockSpec((None, tm, D), row),
            pl.BlockSpec((1, D), const),
            pl.BlockSpec(w_in.shape, const),
            pl.BlockSpec(wvt.shape, const),
            pl.BlockSpec((tm, LANES), tab),
            pl.BlockSpec((tm, LANES), tab),
            pl.BlockSpec((tm, LANES), tab),
        ],
        out_specs=[pl.BlockSpec((None, tm, ATTN_WIDTH), row)] * 2
        + [pl.BlockSpec((None, ATTN_HEADS * VT_ROWS, tm), lambda si, b: (b, 0, si))]
        + [pl.BlockSpec((None, tm, LRU_WIDTH), row)] * 2,
        out_shape=[out, out, jax.ShapeDtypeStruct((B, ATTN_HEADS * VT_ROWS, S), BF16), out, out],
        compiler_params=_cparams(("arbitrary", "arbitrary")),
        name="inproj",
    )(x, g.reshape(1, D), w_in, wvt, c, slo, shi)


def _attn_kernel(lam_ref, q_ref, k_ref, vt_ref, sg_ref, o_ref, *, kc, lookahead):
    lv = lam_ref[...]
    lam = (jnp.exp(jnp.sum(lv[0:1] * lv[1:2], axis=-1, keepdims=True))
           - jnp.exp(jnp.sum(lv[2:3] * lv[3:4], axis=-1, keepdims=True)) + LAMBDA_INIT)
    q = q_ref[...]
    tq = q.shape[0]
    S = k_ref.shape[0]
    lane = lax.broadcasted_iota(jnp.int32, q.shape, 1)
    zero = jnp.zeros_like(q)
    qs = (jnp.where(lane < HEAD_DIM, q, zero), jnp.where(lane >= HEAD_DIM, q, zero))
    m = [jnp.full((1, tq), -1e30, F32) for _ in qs]
    acc = [jnp.zeros((VT_ROWS, tq), F32) for _ in qs]
    n_chunks = S // kc

    def scores(c):
        k_c = k_ref[c * kc:(c + 1) * kc, :]
        return [lax.dot_general(k_c, qm, (((1,), (1,)), ((), ())), preferred_element_type=F32) for qm in qs]

    ahead = [scores(c) for c in range(min(lookahead, n_chunks))]
    for c in range(n_chunks):
        if c + lookahead < n_chunks:
            ahead.append(scores(c + lookahead))
        vt_c = vt_ref[:, c * kc:(c + 1) * kc]
        for j, s in enumerate(ahead.pop(0)):
            m_new = jnp.maximum(m[j], jnp.max(s, axis=0, keepdims=True))
            alpha = jnp.exp2(m[j] - m_new)
            p = jnp.exp2((s - m_new).astype(BF16))
            acc[j] = alpha * acc[j] + jnp.dot(vt_c, p, preferred_element_type=F32)
            m[j] = m_new
    o = (acc[0][:V_DIM] / acc[0][V_DIM:V_DIM + 1]
         - lam * (acc[1][:V_DIM] / acc[1][V_DIM:V_DIM + 1]))
    ms = jnp.mean(o * o, axis=0, keepdims=True)
    o = (o * lax.rsqrt(ms + EPS)).T
    o_ref[...] = (o * sg_ref[...] * (1.0 - LAMBDA_INIT)).astype(o_ref.dtype)


def _attention(q, k, vt, lam_vecs, subln_g, tq, kc):
    B, S, _ = q.shape
    qmap = lambda b, h, i: (b, i, h)
    const = lambda b, h, i: (0, 0)
    return pl.pallas_call(
        functools.partial(_attn_kernel, kc=kc, lookahead=2),
        grid=(B, ATTN_HEADS, S // tq),
        in_specs=[
            pl.BlockSpec((SUBLANES, LANES), const),
            pl.BlockSpec((None, tq, V_DIM), qmap),
            pl.BlockSpec((None, S, V_DIM), lambda b, h, i: (b, 0, h)),
            pl.BlockSpec((None, VT_ROWS, S), lambda b, h, i: (b, h, 0)),
            pl.BlockSpec((1, V_DIM), const),
        ],
        out_specs=pl.BlockSpec((None, tq, V_DIM), qmap),
        out_shape=jax.ShapeDtypeStruct((B, S, ATTN_WIDTH), BF16),
        compiler_params=_cparams(("arbitrary", "arbitrary", "arbitrary")),
        name="diffattn",
    )(lam_vecs, q, k, vt, subln_g.reshape(1, V_DIM))


def _shift_rows(x, d, fill, up):
    n = x.shape[0]
    row = lax.broadcasted_iota(jnp.int32, x.shape, 0)
    if up:
        return jnp.where(row < n - d, pltpu.roll(x, n - d, 0), fill)
    return jnp.where(row >= d, pltpu.roll(x, d, 0), fill)


def _chunk_scan(a, u, reverse):
    n = a.shape[0]
    d = 1
    while d < n:
        a_s = _shift_rows(a, d, 1.0, reverse)
        u_s = _shift_rows(u, d, 0.0, reverse)
        u = a * u_s + u
        a = a * a_s
        d *= 2
    return a, u


def _rglru_kernel(xr_ref, gate_ref, cw_ref, cb_ref, wg_ref, bg_ref, coef_ref, o_ref, xp_ref, hf_ref, *, tc):
    S, C = xr_ref.shape
    pad = SUBLANES
    n_chunks = S // tc
    xp_ref[0:pad, :] = jnp.zeros((pad, C), F32)
    xp_ref[pad + S:pad + S + pad, :] = jnp.zeros((pad, C), F32)
    xp_ref[pad:pad + S, :] = xr_ref[...].astype(F32)
    cw = cw_ref[...]
    cb = cb_ref[...]
    wg = wg_ref[...]
    bg = bg_ref[...]
    coef = coef_ref[...]

    def gates(r0, direction):
        w = xp_ref[pl.ds(r0, tc + 2 * pad), :]
        n = tc + 2 * pad
        xc = (w * cw[2:3] + pltpu.roll(w, 2, 0) * cw[0:1] + pltpu.roll(w, 1, 0) * cw[1:2]
              + pltpu.roll(w, n - 1, 0) * cw[3:4])[pad:pad + tc] + cb
        lin = jnp.dot(xc.astype(BF16), wg[:, 2 * C * direction:2 * C * (direction + 1)],
                      preferred_element_type=F32)
        r = jax.nn.sigmoid(lin[:, :C] + bg[2 * direction:2 * direction + 1])
        i = jax.nn.sigmoid(lin[:, C:] + bg[2 * direction + 1:2 * direction + 2])
        log_a = coef[direction:direction + 1] * r
        a = jnp.exp(log_a)
        th = jnp.tanh(log_a)
        u = jnp.sqrt(-2.0 * th / (1.0 - th)) * (i * xc)
        return a, u

    def fwd(ci, h_prev):
        r0 = pl.multiple_of(ci * tc, tc)
        a, u = gates(r0, 0)
        a_c, h0 = _chunk_scan(a, u, False)
        h = a_c * h_prev + h0
        hf_ref[pl.ds(r0, tc), :] = h
        return jnp.broadcast_to(h[tc - 1:tc], h_prev.shape)

    lax.fori_loop(0, n_chunks, fwd, jnp.zeros((tc, C), F32))

    def bwd(cj, h_next):
        r0 = pl.multiple_of((n_chunks - 1 - cj) * tc, tc)
        a, u = gates(r0, 1)
        a_c, h0 = _chunk_scan(a, u, True)
        h = a_c * h_next + h0
        g = gate_ref[pl.ds(r0, tc), :].astype(F32)
        o_ref[pl.ds(r0, tc), :] = ((hf_ref[pl.ds(r0, tc), :] + h) * jax.nn.gelu(g)).astype(o_ref.dtype)
        return jnp.broadcast_to(h[0:1], h_next.shape)

    lax.fori_loop(0, n_chunks, bwd, jnp.zeros((tc, C), F32))


def _block_diag(w):
    nb, d, _ = w.shape
    eye = jnp.eye(nb, dtype=w.dtype)
    return (eye[:, None, :, None] * w[:, :, None, :]).reshape(nb * d, nb * d)


def _rglru(xr, gate, conv_w, conv_b, w_r, b_r, w_i, b_i, lru_lambda, tc):
    B, S, W = xr.shape
    C = MXU_DIM
    n_c = W // C
    mats = [_block_diag(w_r[0]), _block_diag(w_i[0]), _block_diag(w_r[1]), _block_diag(w_i[1])]
    wg = jnp.stack([jnp.concatenate([m[c * C:(c + 1) * C, c * C:(c + 1) * C] for m in mats], axis=1)
                    for c in range(n_c)]).astype(BF16)
    bg = jnp.stack([b_r[0], b_i[0], b_r[1], b_i[1]])
    coef = -LRU_C * jax.nn.softplus(-lru_lambda.astype(F32))
    blk = lambda b, c: (b, 0, c)
    par = lambda b, c: (0, c)
    return pl.pallas_call(
        functools.partial(_rglru_kernel, tc=tc),
        grid=(B, n_c),
        in_specs=[
            pl.BlockSpec((None, S, C), blk),
            pl.BlockSpec((None, S, C), blk),
            pl.BlockSpec((CONV_WIDTH, C), par),
            pl.BlockSpec((1, C), par),
            pl.BlockSpec((None, C, 4 * C), lambda b, c: (c, 0, 0)),
            pl.BlockSpec((4, C), par),
            pl.BlockSpec((2, C), par),
        ],
        out_specs=pl.BlockSpec((None, S, C), blk),
        out_shape=jax.ShapeDtypeStruct((B, S, W), BF16),
        scratch_shapes=[pltpu.VMEM((S + 2 * SUBLANES, C), F32), pltpu.VMEM((S, C), F32)],
        compiler_params=_cparams(("arbitrary", "arbitrary")),
        name="rglru",
    )(xr, gate, conv_w, conv_b.reshape(1, W), wg, bg, coef)


GRP_LANE0 = N_EXPERTS
R_ID, R_W, R_RANK = 0, 2, 4


def _outproj_kernel(attn_ref, rnn_ref, x_ref, wa_ref, wr_ref, g_ref, wrt_ref, tri_ref,
                    x2_ref, h2_ref, route_ref, route_t_ref, cnt_ref, run_ref):
    @pl.when(pl.program_id(0) == 0)
    def _():
        run_ref[...] = jnp.zeros_like(run_ref)

    mix = (jnp.dot(attn_ref[...], wa_ref[...], preferred_element_type=F32)
           + jnp.dot(rnn_ref[...], wr_ref[...], preferred_element_type=F32))
    x2 = x_ref[...] + mix
    x2_ref[...] = x2
    ms = jnp.mean(x2 * x2, axis=-1, keepdims=True)
    h2 = x2 * lax.rsqrt(ms + EPS) * g_ref[...]
    h2_ref[...] = h2
    logits = jnp.dot(h2.astype(BF16), wrt_ref[...], preferred_element_type=F32)
    lane = lax.broadcasted_iota(jnp.int32, logits.shape, 1)
    neg = jnp.float32(-jnp.inf)
    big = jnp.int32(LANES)
    is_g = (lane >= GRP_LANE0) & (lane < GRP_LANE0 + N_GROUPS)
    gl = jnp.where(is_g, logits, neg)
    gmax = jnp.max(gl, axis=-1, keepdims=True)
    gsum = jnp.sum(jnp.where(is_g, jnp.exp(gl - gmax), 0.0), axis=-1, keepdims=True)
    g_top_p = 1.0 / gsum
    g_idx = jnp.min(jnp.where(is_g & (gl == gmax), lane, big), axis=-1, keepdims=True) - GRP_LANE0
    e_lo = g_idx * EXPERTS_PER_GROUP
    in_grp = (lane >= e_lo) & (lane < e_lo + EXPERTS_PER_GROUP)
    el = jnp.where(in_grp, logits, neg)
    e1 = jnp.max(el, axis=-1, keepdims=True)
    i1 = jnp.min(jnp.where(in_grp & (el == e1), lane, big), axis=-1, keepdims=True)
    el2 = jnp.where(lane == i1, neg, el)
    e2 = jnp.max(el2, axis=-1, keepdims=True)
    i2 = jnp.min(jnp.where(in_grp & (lane != i1) & (el2 == e2), lane, big), axis=-1, keepdims=True)
    t = jnp.exp(e2 - e1)
    w1 = g_top_p / (1.0 + t)
    w2 = g_top_p * t / (1.0 + t)
    oh1 = lane == i1
    oh2 = lane == i2
    oh = jnp.where(oh1 | oh2, 1.0, 0.0)
    before = jnp.dot(tri_ref[...], oh.astype(BF16), preferred_element_type=F32) + run_ref[...]
    rank1 = jnp.sum(jnp.where(oh1, before, 0.0), axis=-1, keepdims=True)
    rank2 = jnp.sum(jnp.where(oh2, before, 0.0), axis=-1, keepdims=True)
    run = run_ref[...] + jnp.sum(oh, axis=0, keepdims=True)
    run_ref[...] = run
    cnt_ref[...] = jnp.broadcast_to(run, cnt_ref.shape)
    route = jnp.where(lane == R_ID, i1.astype(F32), 0.0)
    route = jnp.where(lane == R_ID + 1, i2.astype(F32), route)
    route = jnp.where(lane == R_W, w1, route)
    route = jnp.where(lane == R_W + 1, w2, route)
    route = jnp.where(lane == R_RANK, rank1, route)
    route = jnp.where(lane == R_RANK + 1, rank2, route)
    route_ref[...] = route
    route_t_ref[...] = route.T[0:SUBLANES]


def _outproj_router(attn, rnn, x, w_out, g2, w_grp, w_exp, tm):
    N, D = x.shape
    w_out = w_out.astype(BF16)
    wrt = jnp.zeros((D, LANES), F32).at[:, :N_EXPERTS].set(w_exp).at[:, GRP_LANE0:GRP_LANE0 + N_GROUPS].set(w_grp)
    tri = jnp.tril(jnp.ones((tm, tm), BF16), -1)
    row = lambda i: (i, 0)
    const = lambda i: (0, 0)
    return pl.pallas_call(
        _outproj_kernel,
        grid=(N // tm,),
        in_specs=[
            pl.BlockSpec((tm, ATTN_WIDTH), row),
            pl.BlockSpec((tm, LRU_WIDTH), row),
            pl.BlockSpec((tm, D), row),
            pl.BlockSpec((ATTN_WIDTH, D), const),
            pl.BlockSpec((LRU_WIDTH, D), const),
            pl.BlockSpec((1, D), const),
            pl.BlockSpec((D, LANES), const),
            pl.BlockSpec((tm, tm), const),
        ],
        out_specs=[pl.BlockSpec((tm, D), row), pl.BlockSpec((tm, D), row), pl.BlockSpec((tm, LANES), row),
                   pl.BlockSpec((SUBLANES, tm), lambda i: (0, i)), pl.BlockSpec((SUBLANES, LANES), const)],
        out_shape=[jax.ShapeDtypeStruct((N, D), F32), jax.ShapeDtypeStruct((N, D), F32),
                   jax.ShapeDtypeStruct((N, LANES), F32), jax.ShapeDtypeStruct((SUBLANES, N), F32),
                   jax.ShapeDtypeStruct((SUBLANES, LANES), F32)],
        scratch_shapes=[pltpu.VMEM((1, LANES), F32)],
        compiler_params=_cparams(("arbitrary",)),
        name="outproj_router",
    )(attn, rnn, x, w_out[:ATTN_WIDTH], w_out[ATTN_WIDTH:], g2.reshape(1, D), wrt.astype(BF16), tri)


def _dispatch_plan(route_t, counts, tg):
    N = route_t.shape[1]
    ids = route_t[R_ID:R_ID + 2].astype(jnp.int32)
    rank = route_t[R_RANK:R_RANK + 2].astype(jnp.int32)
    cnt = counts[0, :N_EXPERTS].astype(jnp.int32)
    tiles_e = (cnt + tg - 1) // tg
    tile_end = jnp.cumsum(tiles_e)
    row_off = (tile_end - tiles_e) * tg
    pos = rank
    for e in range(N_EXPERTS):
        pos = pos + jnp.where(ids == e, row_off[e], 0)
    n_tiles = (2 * N) // tg + N_EXPERTS
    n_act = tile_end[-1]
    tile_id = jnp.minimum(jnp.arange(n_tiles, dtype=jnp.int32), n_act - 1)
    tile_expert = jnp.sum((tile_id[:, None] >= tile_end[None, :]).astype(jnp.int32), axis=1)
    pad_start = row_off + cnt
    pad_n = tiles_e * tg - cnt
    return pos[0], pos[1], tile_expert, n_act.reshape(1).astype(jnp.int32), pad_start, pad_n, n_tiles


def _row_copy(src, src_row, dst, dst_row, sem):
    return pltpu.make_async_copy(src.at[pl.ds(src_row, 1)], dst.at[pl.ds(dst_row, 1)], sem)


def _scatter_kernel(pad_start_ref, pad_n_ref, nact_ref, pos0_ref, pos1_ref, h_ref, xs_ref,
                    zero_ref, stage_ref, sem, sems, *, tg):
    tm = h_ref.shape[0]
    half = tg // 2

    @pl.when(pl.program_id(0) == 0)
    def _():
        zero_ref[...] = jnp.zeros_like(zero_ref)

        def fill_tile(j, c):
            base = pl.multiple_of(j * tg, tg)
            cps = [pltpu.make_async_copy(zero_ref, xs_ref.at[pl.ds(base + h * half, half)], sem) for h in range(2)]
            for cp in cps:
                cp.start()
            for cp in cps:
                cp.wait()
            return c

        lax.fori_loop(nact_ref[0], xs_ref.shape[0] // tg, fill_tile, 0)

    @pl.when(pl.program_id(0) == 0)
    def _():
        for e in range(N_EXPERTS):
            n = pad_n_ref[e]
            off = pad_start_ref[e]
            b = 1
            while b < tg:
                hit = (n & b) != 0

                @pl.when(hit)
                def _(off=off, b=b):
                    if b < SUBLANES:
                        cps = [_row_copy(zero_ref, 0, xs_ref, off + i, sem) for i in range(b)]
                    else:
                        cps = [pltpu.make_async_copy(zero_ref.at[pl.ds(0, b)],
                                                     xs_ref.at[pl.ds(pl.multiple_of(off, SUBLANES), b)], sem)]
                    for cp in cps:
                        cp.start()
                    for cp in cps:
                        cp.wait()

                off = off + jnp.where(hit, b, 0)
                b *= 2

    step = pl.program_id(0)
    slot = step % 2
    stage = stage_ref.at[slot]
    stage[...] = h_ref[...]

    def issue(r, c):
        _row_copy(stage, r, xs_ref, pos0_ref[r], sems.at[slot]).start()
        _row_copy(stage, r, xs_ref, pos1_ref[r], sems.at[slot]).start()
        return c

    lax.fori_loop(0, tm, issue, 0, unroll=8)

    def drain(which):
        def body(r, c):
            _row_copy(stage, 0, xs_ref, 0, sems.at[which]).wait()
            _row_copy(stage, 0, xs_ref, 0, sems.at[which]).wait()
            return c

        lax.fori_loop(0, tm, body, 0, unroll=8)

    @pl.when(step > 0)
    def _():
        drain(1 - slot)

    @pl.when(step == pl.num_programs(0) - 1)
    def _():
        drain(slot)


def _scatter_rows(h2, pos0, pos1, pad_start, pad_n, n_act, n_rows, tm, tg):
    N, D = h2.shape
    tok = lambda i, ps, pn, na: (i,)
    return pl.pallas_call(
        functools.partial(_scatter_kernel, tg=tg),
        grid_spec=pltpu.PrefetchScalarGridSpec(
            num_scalar_prefetch=3,
            grid=(N // tm,),
            in_specs=[
                pl.BlockSpec((tm,), tok, memory_space=pltpu.SMEM),
                pl.BlockSpec((tm,), tok, memory_space=pltpu.SMEM),
                pl.BlockSpec((tm, D), lambda i, ps, pn, na: (i, 0)),
            ],
            out_specs=pl.BlockSpec(memory_space=pl.ANY),
            scratch_shapes=[pltpu.VMEM((tg // 2, D), F32), pltpu.VMEM((2, tm, D), F32),
                            pltpu.SemaphoreType.DMA, pltpu.SemaphoreType.DMA((2,))],
        ),
        out_shape=jax.ShapeDtypeStruct((n_rows, D), F32),
        compiler_params=_cparams(("arbitrary",)),
        name="moe_scatter",
    )(pad_start, pad_n, n_act, pos0, pos1, h2)


def _experts_kernel(te_ref, nact_ref, x_ref, wg_ref, wu_ref, wd_ref, y_ref, wg_s, wu_s, wd_s):
    j = pl.program_id(0)
    active = j < nact_ref[0]
    new_expert = (j == 0) | (te_ref[j] != te_ref[jnp.maximum(j - 1, 0)])

    @pl.when(active & new_expert)
    def _():
        wg_s[...] = wg_ref[...].astype(BF16)
        wu_s[...] = wu_ref[...].astype(BF16)
        wd_s[...] = wd_ref[...].astype(BF16)

    @pl.when(active)
    def _():
        x = x_ref[...].astype(BF16)
        gt = jnp.dot(x, wg_s[...], preferred_element_type=F32)
        up = jnp.dot(x, wu_s[...], preferred_element_type=F32)
        hid = (gt * jax.nn.sigmoid(gt)) * up
        y_ref[...] = jnp.dot(hid.astype(BF16), wd_s[...], preferred_element_type=F32)

    @pl.when(jnp.logical_not(active))
    def _():
        y_ref[...] = jnp.zeros_like(y_ref)


def _experts(xs, tile_expert, n_act, w_gate, w_up, w_down, tg):
    R, D = xs.shape
    rows = lambda j, te, na: (j, 0)
    wmap = lambda j, te, na: (te[j], 0, 0)
    return pl.pallas_call(
        _experts_kernel,
        grid_spec=pltpu.PrefetchScalarGridSpec(
            num_scalar_prefetch=2,
            grid=(R // tg,),
            in_specs=[
                pl.BlockSpec((tg, D), rows),
                pl.BlockSpec((None, D, D_EXPERT), wmap),
                pl.BlockSpec((None, D, D_EXPERT), wmap),
                pl.BlockSpec((None, D_EXPERT, D), wmap),
            ],
            out_specs=pl.BlockSpec((tg, D), rows),
            scratch_shapes=[pltpu.VMEM((D, D_EXPERT), BF16), pltpu.VMEM((D, D_EXPERT), BF16),
                            pltpu.VMEM((D_EXPERT, D), BF16)],
        ),
        out_shape=jax.ShapeDtypeStruct((R, D), F32),
        compiler_params=_cparams(("arbitrary",)),
        name="moe_experts",
    )(tile_expert, n_act, xs, w_gate, w_up, w_down)


def _combine_kernel(pos0_ref, pos1_ref, pos0n_ref, pos1n_ref, route_ref, x2_ref, fg_ref, ys_ref, o_ref,
                    buf_ref, sems):
    tm = x2_ref.shape[0]
    step = pl.program_id(0)
    slot = step % 2

    def issue(p0_ref, p1_ref, which):
        def body(r, c):
            _row_copy(ys_ref, p0_ref[r], buf_ref.at[which, 0], r, sems.at[which]).start()
            _row_copy(ys_ref, p1_ref[r], buf_ref.at[which, 1], r, sems.at[which]).start()
            return c

        lax.fori_loop(0, tm, body, 0, unroll=8)

    @pl.when(step == 0)
    def _():
        issue(pos0_ref, pos1_ref, slot)

    @pl.when(step + 1 < pl.num_programs(0))
    def _():
        issue(pos0n_ref, pos1n_ref, 1 - slot)

    def drain(r, c):
        _row_copy(ys_ref, 0, buf_ref.at[slot, 0], 0, sems.at[slot]).wait()
        _row_copy(ys_ref, 0, buf_ref.at[slot, 0], 0, sems.at[slot]).wait()
        return c

    lax.fori_loop(0, tm, drain, 0, unroll=8)
    route = route_ref[...]
    lane = lax.broadcasted_iota(jnp.int32, route.shape, 1)
    w1 = jnp.sum(jnp.where(lane == R_W, route, 0.0), axis=-1, keepdims=True)
    w2 = jnp.sum(jnp.where(lane == R_W + 1, route, 0.0), axis=-1, keepdims=True)
    y = x2_ref[...] + w1 * buf_ref[slot, 0] + w2 * buf_ref[slot, 1]
    ms = jnp.mean(y * y, axis=-1, keepdims=True)
    o_ref[...] = y * lax.rsqrt(ms + EPS) * fg_ref[...]


def _combine(ys, pos0, pos1, route, x2, final_g, tm):
    N, D = x2.shape
    n = N // tm
    row = lambda i: (i, 0)
    cur = lambda i: (i,)
    nxt = lambda i: (jnp.minimum(i + 1, n - 1),)
    smem = functools.partial(pl.BlockSpec, (tm,), memory_space=pltpu.SMEM)
    return pl.pallas_call(
        _combine_kernel,
        grid=(n,),
        in_specs=[
            smem(cur), smem(cur), smem(nxt), smem(nxt),
            pl.BlockSpec((tm, LANES), row),
            pl.BlockSpec((tm, D), row),
            pl.BlockSpec((1, D), lambda i: (0, 0)),
            pl.BlockSpec(memory_space=pl.ANY),
        ],
        out_specs=pl.BlockSpec((tm, D), row),
        out_shape=jax.ShapeDtypeStruct((N, D), F32),
        scratch_shapes=[pltpu.VMEM((2, 2, tm, D), F32), pltpu.SemaphoreType.DMA((2,))],
        compiler_params=_cparams(("arbitrary",)),
        name="moe_combine",
    )(pos0, pos1, pos0, pos1, route, x2, final_g.reshape(1, D), ys)


def _tiles(B, S):
    return dict(
        tm_in=min(512, S),
        tq=min(512, S),
        kc=min(256, S),
        tc=min(64, S),
        tm_out=min(512, S),
        tm_row=min(256, S),
        tg=min(512, S),
    )


def kernel(x, norm1_g, w_in, lambda_q1, lambda_k1, lambda_q2, lambda_k2, subln_g, conv_w, conv_b,
           lru_w_r, lru_b_r, lru_w_i, lru_b_i, lru_lambda, w_out, norm2_g, w_grp, w_exp, w_gate, w_up,
           w_down, final_g):
    B, S, D = x.shape
    t = _tiles(B, S)
    l = 0
    q, k, vt, xr, gate = _inproj(x, norm1_g[l], w_in[l], t["tm_in"])
    lam_vecs = jnp.zeros((SUBLANES, LANES), F32).at[0:4, :HEAD_DIM].set(
        jnp.stack([lambda_q1[l], lambda_k1[l], lambda_q2[l], lambda_k2[l]]))
    attn = _attention(q, k, vt, lam_vecs, subln_g[l], t["tq"], t["kc"])
    rnn = _rglru(xr, gate, conv_w[l], conv_b[l], lru_w_r[l], lru_b_r[l], lru_w_i[l], lru_b_i[l],
                 lru_lambda[l], t["tc"])
    N = B * S
    x2, h2, route, route_t, counts = _outproj_router(
        attn.reshape(N, ATTN_WIDTH), rnn.reshape(N, LRU_WIDTH), x.reshape(N, D), w_out[l], norm2_g[l],
        w_grp[l], w_exp[l], t["tm_out"])
    tg = t["tg"]
    pos0, pos1, tile_expert, n_act, pad_start, pad_n, n_tiles = _dispatch_plan(route_t, counts, tg)
    xs = _scatter_rows(h2, pos0, pos1, pad_start, pad_n, n_act, n_tiles * tg, t["tm_row"], tg)
    ys = _experts(xs, tile_expert, n_act, w_gate[l], w_up[l], w_down[l], tg)
    out = _combine(ys, pos0, pos1, route, x2, final_g, t["tm_row"])
    return out.reshape(B, S, D)
```

```python
import functools
import math

import jax
import jax.numpy as jnp
from jax import lax
from jax.experimental import pallas as pl
from jax.experimental.pallas import tpu as pltpu

F32 = jnp.float32
BF16 = jnp.bfloat16

D_MODEL = 1024
ATTN_HEADS = 4
HEAD_DIM = 64
V_DIM = 2 * HEAD_DIM
ATTN_WIDTH = ATTN_HEADS * V_DIM
ROPE_THETA = 500000.0
ROPE_DIM = HEAD_DIM // 4
LRU_WIDTH = D_MODEL - ATTN_WIDTH
LRU_BLOCK_DIM = 64
CONV_WIDTH = 4
LRU_C = 8.0
N_GROUPS = 4
EXPERTS_PER_GROUP = 8
N_EXPERTS = N_GROUPS * EXPERTS_PER_GROUP
D_EXPERT = 256
EPS = 1e-6
SQRT_GUARD = 1e-37
LAMBDA_INIT = 0.8 - 0.6 * math.exp(0.0)

LANES = 128
SUBLANES = 8
MXU_DIM = 256
BF16_SUBLANES = 16
VT_ROWS = V_DIM + BF16_SUBLANES
VMEM_LIMIT = 56 * 1024 * 1024


def _cparams(semantics, flags=None):
    return pltpu.CompilerParams(dimension_semantics=semantics, vmem_limit_bytes=VMEM_LIMIT, flags=flags)


def _rope_tables(seq_len):
    pos = jnp.arange(seq_len, dtype=F32)
    inv_freq = ROPE_THETA ** (-jnp.arange(0, ROPE_DIM, 2, dtype=F32) / ROPE_DIM)
    ang = pos[:, None] * inv_freq[None, :]
    cos, sin = jnp.cos(ang), jnp.sin(ang)
    half = ROPE_DIM // 2
    j = jnp.arange(LANES) % HEAD_DIM
    f = j % half
    c = jnp.where(j[None, :] < ROPE_DIM, cos[:, f], 1.0)
    s_lo = jnp.where(j[None, :] < half, -sin[:, f], 0.0)
    s_hi = jnp.where((j[None, :] >= half) & (j[None, :] < ROPE_DIM), sin[:, f], 0.0)
    return c, s_lo, s_hi


def _inproj_kernel(x_ref, g_ref, w_ref, wvt_ref, c_ref, slo_ref, shi_ref,
                   q_ref, k_ref, vt_ref, xr_ref, gate_ref):
    x = x_ref[...]
    ms = jnp.mean(x * x, axis=-1, keepdims=True)
    h = (x * lax.rsqrt(ms + EPS) * g_ref[...]).astype(BF16)
    c, slo, shi = c_ref[...], slo_ref[...], shi_ref[...]

    def rope(p):
        return p * c + pltpu.roll(p, LANES - ROPE_DIM // 2, 1) * slo + pltpu.roll(p, ROPE_DIM // 2, 1) * shi

    q_scale = HEAD_DIM ** -0.5 * math.log2(math.e)
    pq = jnp.dot(h, w_ref[:, 0:ATTN_WIDTH], preferred_element_type=F32)
    pk = jnp.dot(h, w_ref[:, ATTN_WIDTH:2 * ATTN_WIDTH], preferred_element_type=F32)
    for hd in range(ATTN_HEADS):
        lo = hd * V_DIM
        q_ref[:, lo:lo + V_DIM] = (rope(pq[:, lo:lo + V_DIM]) * q_scale).astype(BF16)
        k_ref[:, lo:lo + V_DIM] = rope(pk[:, lo:lo + V_DIM]).astype(BF16)
    vt = lax.dot_general(wvt_ref[...], h, (((1,), (1,)), ((), ())), preferred_element_type=F32).astype(BF16)
    for hd in range(ATTN_HEADS):
        vt_ref[hd * VT_ROWS:hd * VT_ROWS + V_DIM, :] = vt[hd * V_DIM:(hd + 1) * V_DIM]
        vt_ref[hd * VT_ROWS + V_DIM:(hd + 1) * VT_ROWS, :] = jnp.ones((VT_ROWS - V_DIM, vt.shape[1]), BF16)
    off = 3 * ATTN_WIDTH
    xr_ref[...] = jnp.dot(h, w_ref[:, off:off + LRU_WIDTH], preferred_element_type=F32).astype(BF16)
    off += LRU_WIDTH
    gate_ref[...] = jnp.dot(h, w_ref[:, off:off + LRU_WIDTH], preferred_element_type=F32).astype(BF16)


def _inproj(x, g, w_in, tm):
    B, S, D = x.shape
    c, slo, shi = _rope_tables(S)
    row = lambda si, b: (b, si, 0)
    tab = lambda si, b: (si, 0)
    const = lambda si, b: (0, 0)
    out = jax.ShapeDtypeStruct((B, S, ATTN_WIDTH), BF16)
    w_in = w_in.astype(BF16)
    wvt = w_in[:, 2 * ATTN_WIDTH:3 * ATTN_WIDTH].T
    return pl.pallas_call(
        _inproj_kernel,
        grid=(S // tm, B),
        in_specs=[
            pl.BlockSpec((None, tm, D), row),
            pl.BlockSpec((1, D), const),
            pl.BlockSpec(w_in.shape, const),
            pl.BlockSpec(wvt.shape, const),
            pl.BlockSpec((tm, LANES), tab),
            pl.BlockSpec((tm, LANES), tab),
            pl.BlockSpec((tm, LANES), tab),
        ],
        out_specs=[pl.BlockSpec((None, tm, ATTN_WIDTH), row)] * 2
        + [pl.BlockSpec((None, ATTN_HEADS * VT_ROWS, tm), lambda si, b: (b, 0, si))]
        + [pl.BlockSpec((None, tm, LRU_WIDTH), row)] * 2,
        out_shape=[out, out, jax.ShapeDtypeStruct((B, ATTN_HEADS * VT_ROWS, S), BF16), out, out],
        compiler_params=_cparams(("arbitrary", "arbitrary")),
        name="inproj",
    )(x, g.reshape(1, D), w_in, wvt, c, slo, shi)


def _attn_kernel(lam_ref, q_ref, k_ref, vt_ref, sg_ref, o_ref, *, kc, lookahead):
    lv = lam_ref[...]
    lam = (jnp.exp(jnp.sum(lv[0:1] * lv[1:2], axis=-1, keepdims=True))
           - jnp.exp(jnp.sum(lv[2:3] * lv[3:4], axis=-1, keepdims=True)) + LAMBDA_INIT)
    q = q_ref[...]
    tq = q.shape[0]
    S = k_ref.shape[0]
    lane = lax.broadcasted_iota(jnp.int32, q.shape, 1)
    zero = jnp.zeros_like(q)
    qs = (jnp.where(lane < HEAD_DIM, q, zero), jnp.where(lane >= HEAD_DIM, q, zero))
    m = [jnp.full((1, tq), -1e30, F32) for _ in qs]
    acc = [jnp.zeros((VT_ROWS, tq), F32) for _ in qs]
    n_chunks = S // kc

    def scores(c):
        k_c = k_ref[c * kc:(c + 1) * kc, :]
        return [lax.dot_general(k_c, qm, (((1,), (1,)), ((), ())), preferred_element_type=F32) for qm in qs]

    ahead = [scores(c) for c in range(min(lookahead, n_chunks))]
    for c in range(n_chunks):
        if c + lookahead < n_chunks:
            ahead.append(scores(c + lookahead))
        vt_c = vt_ref[:, c * kc:(c + 1) * kc]
        for j, s in enumerate(ahead.pop(0)):
            m_new = jnp.maximum(m[j], jnp.max(s, axis=0, keepdims=True))
            alpha = jnp.exp2(m[j] - m_new)
            p = jnp.exp2((s - m_new).astype(BF16))
            acc[j] = alpha * acc[j] + jnp.dot(vt_c, p, preferred_element_type=F32)
            m[j] = m_new
    o = (acc[0][:V_DIM] / acc[0][V_DIM:V_DIM + 1]
         - lam * (acc[1][:V_DIM] / acc[1][V_DIM:V_DIM + 1]))
    ms = jnp.mean(o * o, axis=0, keepdims=True)
    o = (o * lax.rsqrt(ms + EPS)).T
    o_ref[...] = (o * sg_ref[...] * (1.0 - LAMBDA_INIT)).astype(o_ref.dtype)


def _attention(q, k, vt, lam_vecs, subln_g, tq, kc):
    B, S, _ = q.shape
    qmap = lambda b, h, i: (b, i, h)
    const = lambda b, h, i: (0, 0)
    return pl.pallas_call(
        functools.partial(_attn_kernel, kc=kc, lookahead=2),
        grid=(B, ATTN_HEADS, S // tq),
        in_specs=[
            pl.BlockSpec((SUBLANES, LANES), const),
            pl.BlockSpec((None, tq, V_DIM), qmap),
            pl.BlockSpec((None, S, V_DIM), lambda b, h, i: (b, 0, h)),
            pl.BlockSpec((None, VT_ROWS, S), lambda b, h, i: (b, h, 0)),
            pl.BlockSpec((1, V_DIM), const),
        ],
        out_specs=pl.BlockSpec((None, tq, V_DIM), qmap),
        out_shape=jax.ShapeDtypeStruct((B, S, ATTN_WIDTH), BF16),
        compiler_params=_cparams(("arbitrary", "arbitrary", "arbitrary")),
        name="diffattn",
    )(lam_vecs, q, k, vt, subln_g.reshape(1, V_DIM))


def _shift_rows(x, d, fill, up):
    n, c = x.shape
    if d % SUBLANES == 0:
        pad = jnp.full((d, c), fill, x.dtype)
        return jnp.concatenate([x[d:], pad]) if up else jnp.concatenate([pad, x[:n - d]])
    row = lax.broadcasted_iota(jnp.int32, (SUBLANES, c), 0)
    if up:
        r = pltpu.roll(x, n - d, 0)
        last = jnp.where(row < SUBLANES - d, r[n - SUBLANES:], fill)
        return last if n == SUBLANES else jnp.concatenate([r[:n - SUBLANES], last])
    r = pltpu.roll(x, d, 0)
    first = jnp.where(row >= d, r[:SUBLANES], fill)
    return first if n == SUBLANES else jnp.concatenate([first, r[SUBLANES:]])


def _chunk_scan(a, u, reverse):
    n = a.shape[0]
    d = 1
    while d < n:
        a_s = _shift_rows(a, d, 1.0, reverse)
        u_s = _shift_rows(u, d, 0.0, reverse)
        u = a * u_s + u
        a = a * a_s
        d *= 2
    return a, u


def _rglru_kernel(xr_ref, gate_ref, cw_ref, cb_ref, wg_ref, bg_ref, coef_ref, o_ref,
                  xp_ref, gp_ref, hf_ref, pf_ref, hb_ref, pb_ref, *, tcp):
    S, C = xr_ref.shape
    SEG = SUBLANES
    L = S // SEG
    blk = BF16_SUBLANES
    n = tcp * SEG
    halo_lo = 2 * SEG

    def permute_in(b, c):
        for j in range(SEG):
            src = pl.multiple_of(j * L + b * blk, blk)
            dst = (b * blk) * SEG + j
            xp_ref[pl.ds(halo_lo + dst, blk, stride=SEG), :] = xr_ref[pl.ds(src, blk), :].astype(F32)
            gp_ref[pl.ds(dst, blk, stride=SEG), :] = gate_ref[pl.ds(src, blk), :].astype(F32)
        return c

    lax.fori_loop(0, L // blk, permute_in, 0)
    row = lax.broadcasted_iota(jnp.int32, (SEG, C), 0)
    for k in (1, 2):
        tail = xp_ref[pl.ds(halo_lo + (L - k) * SEG, SEG), :]
        xp_ref[pl.ds(halo_lo - k * SEG, SEG), :] = jnp.where(row >= 1, pltpu.roll(tail, 1, 0), 0.0)
    head = xp_ref[pl.ds(halo_lo, SEG), :]
    xp_ref[pl.ds(halo_lo + L * SEG, SEG), :] = jnp.where(row < SEG - 1, pltpu.roll(head, SEG - 1, 0), 0.0)

    cw = cw_ref[...]
    cb = cb_ref[...]
    bg = bg_ref[...]
    coef = coef_ref[...]

    def decay_and_input(r0, direction):
        w = xp_ref[pl.ds(r0, n + 3 * SEG), :]
        xc = (w[0:n] * cw[0:1] + w[SEG:SEG + n] * cw[1:2] + w[2 * SEG:2 * SEG + n] * cw[2:3]
              + w[3 * SEG:3 * SEG + n] * cw[3:4]) + cb
        lin = jnp.dot(xc.astype(BF16), wg_ref[:, 2 * C * direction:2 * C * (direction + 1)],
                      preferred_element_type=F32)
        r = jax.nn.sigmoid(lin[:, :C] + bg[2 * direction:2 * direction + 1])
        i = jax.nn.sigmoid(lin[:, C:] + bg[2 * direction + 1:2 * direction + 2])
        log_a = coef[direction:direction + 1] * r
        th = jnp.tanh(log_a)
        one_m_a2 = (-2.0 * th) * pl.reciprocal(1.0 - th, approx=True)
        root = one_m_a2 * lax.rsqrt(jnp.maximum(one_m_a2, SQRT_GUARD))
        return jnp.exp(log_a), root * (i * xc)

    n_chunks = L // tcp

    def scan_chunk(c_idx, direction, h, p, h_ref, p_ref):
        r0 = pl.multiple_of(c_idx * n, n)
        a, u = decay_and_input(r0, direction)
        hs, ps = [None] * tcp, [None] * tcp
        for s in (range(tcp) if direction == 0 else reversed(range(tcp))):
            a_s = a[s * SEG:(s + 1) * SEG]
            h = a_s * h + u[s * SEG:(s + 1) * SEG]
            p = a_s * p
            hs[s], ps[s] = h, p
        h_ref[pl.ds(r0, n), :] = jnp.concatenate(hs)
        p_ref[pl.ds(r0, n), :] = jnp.concatenate(ps)
        return h, p

    def scan_both(ci, carry):
        hf, pf, hb, pb = carry
        hf, pf = scan_chunk(ci, 0, hf, pf, hf_ref, pf_ref)
        hb, pb = scan_chunk(n_chunks - 1 - ci, 1, hb, pb, hb_ref, pb_ref)
        return hf, pf, hb, pb

    zero, one = jnp.zeros((SEG, C), F32), jnp.ones((SEG, C), F32)
    hf_end, pf_end, hb_end, pb_end = lax.fori_loop(0, n_chunks, scan_both, (zero, one, zero, one), unroll=2)
    c_f = _shift_rows(_chunk_scan(pf_end, hf_end, False)[1], 1, 0.0, False)
    c_b = _shift_rows(_chunk_scan(pb_end, hb_end, True)[1], 1, 0.0, True)

    def finish(ci, c):
        rows = pl.ds(pl.multiple_of(ci * n, n), n)
        cf = jnp.concatenate([c_f] * tcp)
        cbk = jnp.concatenate([c_b] * tcp)
        h = hf_ref[rows, :] + pf_ref[rows, :] * cf + hb_ref[rows, :] + pb_ref[rows, :] * cbk
        gp_ref[rows, :] = h * jax.nn.gelu(gp_ref[rows, :])
        return c

    lax.fori_loop(0, L // tcp, finish, 0)

    def permute_out(b, c):
        for j in range(SEG):
            dst = pl.multiple_of(j * L + b * blk, blk)
            o_ref[pl.ds(dst, blk), :] = gp_ref[pl.ds((b * blk) * SEG + j, blk, stride=SEG), :].astype(o_ref.dtype)
        return c

    lax.fori_loop(0, L // blk, permute_out, 0)


def _block_diag(w):
    nb, d, _ = w.shape
    eye = jnp.eye(nb, dtype=w.dtype)
    return (eye[:, None, :, None] * w[:, :, None, :]).reshape(nb * d, nb * d)


def _rglru(xr, gate, conv_w, conv_b, w_r, b_r, w_i, b_i, lru_lambda, tcp):
    B, S, W = xr.shape
    C = LANES
    n_c = W // C
    mats = [_block_diag(w_r[0]), _block_diag(w_i[0]), _block_diag(w_r[1]), _block_diag(w_i[1])]
    wg = jnp.stack([jnp.concatenate([m[c * C:(c + 1) * C, c * C:(c + 1) * C] for m in mats], axis=1)
                    for c in range(n_c)]).astype(BF16)
    bg = jnp.stack([b_r[0], b_i[0], b_r[1], b_i[1]])
    coef = -LRU_C * jax.nn.softplus(-lru_lambda.astype(F32))
    blk = lambda b, c: (b, 0, c)
    par = lambda b, c: (0, c)
    return pl.pallas_call(
        functools.partial(_rglru_kernel, tcp=tcp),
        grid=(B, n_c),
        in_specs=[
            pl.BlockSpec((None, S, C), blk),
            pl.BlockSpec((None, S, C), blk),
            pl.BlockSpec((CONV_WIDTH, C), par),
            pl.BlockSpec((1, C), par),
            pl.BlockSpec((None, C, 4 * C), lambda b, c: (c, 0, 0)),
            pl.BlockSpec((4, C), par),
            pl.BlockSpec((2, C), par),
        ],
        out_specs=pl.BlockSpec((None, S, C), blk),
        out_shape=jax.ShapeDtypeStruct((B, S, W), BF16),
        scratch_shapes=[pltpu.VMEM((S + (CONV_WIDTH - 1) * SUBLANES, C), F32)] + [pltpu.VMEM((S, C), F32)] * 5,
        compiler_params=_cparams(("arbitrary", "arbitrary")),
        name="rglru",
    )(xr, gate, conv_w, conv_b.reshape(1, W), wg, bg, coef)


GRP_LANE0 = N_EXPERTS
R_ID, R_W, R_RANK = 0, 2, 4


def _outproj_kernel(attn_ref, rnn_ref, x_ref, wa_ref, wr_ref, g_ref, wrt_ref, tri_ref,
                    x2_ref, h2_ref, route_ref, route_t_ref, cnt_ref, run_ref):
    @pl.when(pl.program_id(0) == 0)
    def _():
        run_ref[...] = jnp.zeros_like(run_ref)

    mix = (jnp.dot(attn_ref[...], wa_ref[...], preferred_element_type=F32)
           + jnp.dot(rnn_ref[...], wr_ref[...], preferred_element_type=F32))
    x2 = x_ref[...] + mix
    x2_ref[...] = x2
    ms = jnp.mean(x2 * x2, axis=-1, keepdims=True)
    h2 = x2 * lax.rsqrt(ms + EPS) * g_ref[...]
    h2_ref[...] = h2
    logits = jnp.dot(h2.astype(BF16), wrt_ref[...], preferred_element_type=F32)
    lane = lax.broadcasted_iota(jnp.int32, logits.shape, 1)
    neg = jnp.float32(-jnp.inf)
    big = jnp.int32(LANES)
    is_g = (lane >= GRP_LANE0) & (lane < GRP_LANE0 + N_GROUPS)
    gl = jnp.where(is_g, logits, neg)
    gmax = jnp.max(gl, axis=-1, keepdims=True)
    gsum = jnp.sum(jnp.where(is_g, jnp.exp(gl - gmax), 0.0), axis=-1, keepdims=True)
    g_top_p = 1.0 / gsum
    g_idx = jnp.min(jnp.where(is_g & (gl == gmax), lane, big), axis=-1, keepdims=True) - GRP_LANE0
    e_lo = g_idx * EXPERTS_PER_GROUP
    in_grp = (lane >= e_lo) & (lane < e_lo + EXPERTS_PER_GROUP)
    el = jnp.where(in_grp, logits, neg)
    e1 = jnp.max(el, axis=-1, keepdims=True)
    i1 = jnp.min(jnp.where(in_grp & (el == e1), lane, big), axis=-1, keepdims=True)
    el2 = jnp.where(lane == i1, neg, el)
    e2 = jnp.max(el2, axis=-1, keepdims=True)
    i2 = jnp.min(jnp.where(in_grp & (lane != i1) & (el2 == e2), lane, big), axis=-1, keepdims=True)
    t = jnp.exp(e2 - e1)
    w1 = g_top_p / (1.0 + t)
    w2 = g_top_p * t / (1.0 + t)
    oh1 = lane == i1
    oh2 = lane == i2
    oh = jnp.where(oh1 | oh2, 1.0, 0.0)
    before = jnp.dot(tri_ref[...], oh.astype(BF16), preferred_element_type=F32) + run_ref[...]
    rank1 = jnp.sum(jnp.where(oh1, before, 0.0), axis=-1, keepdims=True)
    rank2 = jnp.sum(jnp.where(oh2, before, 0.0), axis=-1, keepdims=True)
    run = run_ref[...] + jnp.sum(oh, axis=0, keepdims=True)
    run_ref[...] = run
    cnt_ref[...] = jnp.broadcast_to(run, cnt_ref.shape)
    route = jnp.where(lane == R_ID, i1.astype(F32), 0.0)
    route = jnp.where(lane == R_ID + 1, i2.astype(F32), route)
    route = jnp.where(lane == R_W, w1, route)
    route = jnp.where(lane == R_W + 1, w2, route)
    route = jnp.where(lane == R_RANK, rank1, route)
    route = jnp.where(lane == R_RANK + 1, rank2, route)
    route_ref[...] = route
    route_t_ref[...] = route.T[0:SUBLANES]


def _outproj_router(attn, rnn, x, w_out, g2, w_grp, w_exp, tm):
    N, D = x.shape
    w_out = w_out.astype(BF16)
    wrt = jnp.zeros((D, LANES), F32).at[:, :N_EXPERTS].set(w_exp).at[:, GRP_LANE0:GRP_LANE0 + N_GROUPS].set(w_grp)
    tri = jnp.tril(jnp.ones((tm, tm), BF16), -1)
    row = lambda i: (i, 0)
    const = lambda i: (0, 0)
    return pl.pallas_call(
        _outproj_kernel,
        grid=(N // tm,),
        in_specs=[
            pl.BlockSpec((tm, ATTN_WIDTH), row),
            pl.BlockSpec((tm, LRU_WIDTH), row),
            pl.BlockSpec((tm, D), row),
            pl.BlockSpec((ATTN_WIDTH, D), const),
            pl.BlockSpec((LRU_WIDTH, D), const),
            pl.BlockSpec((1, D), const),
            pl.BlockSpec((D, LANES), const),
            pl.BlockSpec((tm, tm), const),
        ],
        out_specs=[pl.BlockSpec((tm, D), row), pl.BlockSpec((tm, D), row), pl.BlockSpec((tm, LANES), row),
                   pl.BlockSpec((SUBLANES, tm), lambda i: (0, i)), pl.BlockSpec((SUBLANES, LANES), const)],
        out_shape=[jax.ShapeDtypeStruct((N, D), F32), jax.ShapeDtypeStruct((N, D), F32),
                   jax.ShapeDtypeStruct((N, LANES), F32), jax.ShapeDtypeStruct((SUBLANES, N), F32),
                   jax.ShapeDtypeStruct((SUBLANES, LANES), F32)],
        scratch_shapes=[pltpu.VMEM((1, LANES), F32)],
        compiler_params=_cparams(("arbitrary",)),
        name="outproj_router",
    )(attn, rnn, x, w_out[:ATTN_WIDTH], w_out[ATTN_WIDTH:], g2.reshape(1, D), wrt.astype(BF16), tri)


def _dispatch_plan(route_t, counts, tg):
    N = route_t.shape[1]
    ids = route_t[R_ID:R_ID + 2].astype(jnp.int32)
    rank = route_t[R_RANK:R_RANK + 2].astype(jnp.int32)
    cnt = counts[0, :N_EXPERTS].astype(jnp.int32)
    tiles_e = (cnt + tg - 1) // tg
    tile_end = jnp.cumsum(tiles_e)
    row_off = (tile_end - tiles_e) * tg
    pos = rank
    for e in range(N_EXPERTS):
        pos = pos + jnp.where(ids == e, row_off[e], 0)
    n_tiles = (2 * N) // tg + N_EXPERTS
    n_act = tile_end[-1]
    tile_id = jnp.minimum(jnp.arange(n_tiles, dtype=jnp.int32), n_act - 1)
    tile_expert = jnp.sum((tile_id[:, None] >= tile_end[None, :]).astype(jnp.int32), axis=1)
    pad_start = row_off + cnt
    pad_n = tiles_e * tg - cnt
    return pos[0], pos[1], tile_expert, n_act.reshape(1).astype(jnp.int32), pad_start, pad_n, n_tiles


def _row_copy(src, src_row, dst, dst_row, sem):
    return pltpu.make_async_copy(src.at[pl.ds(src_row, 1)], dst.at[pl.ds(dst_row, 1)], sem)


def _scatter_kernel(pad_start_ref, pad_n_ref, nact_ref, pos0_ref, pos1_ref, h_ref, xs_ref,
                    zero_ref, stage_ref, sem, sems, *, tg):
    tm = h_ref.shape[0]
    half = tg // 2

    @pl.when(pl.program_id(0) == 0)
    def _():
        zero_ref[...] = jnp.zeros_like(zero_ref)

        def fill_tile(j, c):
            base = pl.multiple_of(j * tg, tg)
            cps = [pltpu.make_async_copy(zero_ref, xs_ref.at[pl.ds(base + h * half, half)], sem) for h in range(2)]
            for cp in cps:
                cp.start()
            for cp in cps:
                cp.wait()
            return c

        lax.fori_loop(nact_ref[0], xs_ref.shape[0] // tg, fill_tile, 0)

    @pl.when(pl.program_id(0) == 0)
    def _():
        for e in range(N_EXPERTS):
            n = pad_n_ref[e]
            off = pad_start_ref[e]
            b = 1
            while b < tg:
                hit = (n & b) != 0

                @pl.when(hit)
                def _(off=off, b=b):
                    if b < SUBLANES:
                        cps = [_row_copy(zero_ref, 0, xs_ref, off + i, sem) for i in range(b)]
                    else:
                        cps = [pltpu.make_async_copy(zero_ref.at[pl.ds(0, b)],
                                                     xs_ref.at[pl.ds(pl.multiple_of(off, SUBLANES), b)], sem)]
                    for cp in cps:
                        cp.start()
                    for cp in cps:
                        cp.wait()

                off = off + jnp.where(hit, b, 0)
                b *= 2

    step = pl.program_id(0)
    slot = step % 2
    stage = stage_ref.at[slot]
    stage[...] = h_ref[...]

    def issue(r, c):
        _row_copy(stage, r, xs_ref, pos0_ref[r], sems.at[slot]).start()
        _row_copy(stage, r, xs_ref, pos1_ref[r], sems.at[slot]).start()
        return c

    lax.fori_loop(0, tm, issue, 0, unroll=8)

    def drain(which):
        def body(r, c):
            _row_copy(stage, 0, xs_ref, 0, sems.at[which]).wait()
            _row_copy(stage, 0, xs_ref, 0, sems.at[which]).wait()
            return c

        lax.fori_loop(0, tm, body, 0, unroll=8)

    @pl.when(step > 0)
    def _():
        drain(1 - slot)

    @pl.when(step == pl.num_programs(0) - 1)
    def _():
        drain(slot)


def _scatter_rows(h2, pos0, pos1, pad_start, pad_n, n_act, n_rows, tm, tg):
    N, D = h2.shape
    tok = lambda i, ps, pn, na: (i,)
    return pl.pallas_call(
        functools.partial(_scatter_kernel, tg=tg),
        grid_spec=pltpu.PrefetchScalarGridSpec(
            num_scalar_prefetch=3,
            grid=(N // tm,),
            in_specs=[
                pl.BlockSpec((tm,), tok, memory_space=pltpu.SMEM),
                pl.BlockSpec((tm,), tok, memory_space=pltpu.SMEM),
                pl.BlockSpec((tm, D), lambda i, ps, pn, na: (i, 0)),
            ],
            out_specs=pl.BlockSpec(memory_space=pl.ANY),
            scratch_shapes=[pltpu.VMEM((tg // 2, D), F32), pltpu.VMEM((2, tm, D), F32),
                            pltpu.SemaphoreType.DMA, pltpu.SemaphoreType.DMA((2,))],
        ),
        out_shape=jax.ShapeDtypeStruct((n_rows, D), F32),
        compiler_params=_cparams(("arbitrary",)),
        name="moe_scatter",
    )(pad_start, pad_n, n_act, pos0, pos1, h2)


def _experts_kernel(te_ref, nact_ref, x_ref, wg_ref, wu_ref, wd_ref, y_ref, wg_s, wu_s, wd_s):
    j = pl.program_id(0)
    active = j < nact_ref[0]
    new_expert = (j == 0) | (te_ref[j] != te_ref[jnp.maximum(j - 1, 0)])

    @pl.when(active & new_expert)
    def _():
        wg_s[...] = wg_ref[...].astype(BF16)
        wu_s[...] = wu_ref[...].astype(BF16)
        wd_s[...] = wd_ref[...].astype(BF16)

    @pl.when(active)
    def _():
        x = x_ref[...].astype(BF16)
        gt = jnp.dot(x, wg_s[...], preferred_element_type=F32)
        up = jnp.dot(x, wu_s[...], preferred_element_type=F32)
        hid = (gt * jax.nn.sigmoid(gt)) * up
        y_ref[...] = jnp.dot(hid.astype(BF16), wd_s[...], preferred_element_type=F32)

    @pl.when(jnp.logical_not(active))
    def _():
        y_ref[...] = jnp.zeros_like(y_ref)


def _experts(xs, tile_expert, n_act, w_gate, w_up, w_down, tg):
    R, D = xs.shape
    rows = lambda j, te, na: (j, 0)
    wmap = lambda j, te, na: (te[j], 0, 0)
    return pl.pallas_call(
        _experts_kernel,
        grid_spec=pltpu.PrefetchScalarGridSpec(
            num_scalar_prefetch=2,
            grid=(R // tg,),
            in_specs=[
                pl.BlockSpec((tg, D), rows),
                pl.BlockSpec((None, D, D_EXPERT), wmap),
                pl.BlockSpec((None, D, D_EXPERT), wmap),
                pl.BlockSpec((None, D_EXPERT, D), wmap),
            ],
            out_specs=pl.BlockSpec((tg, D), rows),
            scratch_shapes=[pltpu.VMEM((D, D_EXPERT), BF16), pltpu.VMEM((D, D_EXPERT), BF16),
                            pltpu.VMEM((D_EXPERT, D), BF16)],
        ),
        out_shape=jax.ShapeDtypeStruct((R, D), F32),
        compiler_params=_cparams(("arbitrary",)),
        name="moe_experts",
    )(tile_expert, n_act, xs, w_gate, w_up, w_down)


def _combine_kernel(pos0_ref, pos1_ref, pos0n_ref, pos1n_ref, route_ref, x2_ref, fg_ref, ys_ref, o_ref,
                    buf_ref, sems):
    tm = x2_ref.shape[0]
    step = pl.program_id(0)
    slot = step % 2

    def issue(p0_ref, p1_ref, which):
        def body(r, c):
            _row_copy(ys_ref, p0_ref[r], buf_ref.at[which, 0], r, sems.at[which]).start()
            _row_copy(ys_ref, p1_ref[r], buf_ref.at[which, 1], r, sems.at[which]).start()
            return c

        lax.fori_loop(0, tm, body, 0, unroll=8)

    @pl.when(step == 0)
    def _():
        issue(pos0_ref, pos1_ref, slot)

    @pl.when(step + 1 < pl.num_programs(0))
    def _():
        issue(pos0n_ref, pos1n_ref, 1 - slot)

    def drain(r, c):
        _row_copy(ys_ref, 0, buf_ref.at[slot, 0], 0, sems.at[slot]).wait()
        _row_copy(ys_ref, 0, buf_ref.at[slot, 0], 0, sems.at[slot]).wait()
        return c

    lax.fori_loop(0, tm, drain, 0, unroll=8)
    route = route_ref[...]
    lane = lax.broadcasted_iota(jnp.int32, route.shape, 1)
    w1 = jnp.sum(jnp.where(lane == R_W, route, 0.0), axis=-1, keepdims=True)
    w2 = jnp.sum(jnp.where(lane == R_W + 1, route, 0.0), axis=-1, keepdims=True)
    y = x2_ref[...] + w1 * buf_ref[slot, 0] + w2 * buf_ref[slot, 1]
    ms = jnp.mean(y * y, axis=-1, keepdims=True)
    o_ref[...] = y * lax.rsqrt(ms + EPS) * fg_ref[...]


def _combine(ys, pos0, pos1, route, x2, final_g, tm):
    N, D = x2.shape
    n = N // tm
    row = lambda i: (i, 0)
    cur = lambda i: (i,)
    nxt = lambda i: (jnp.minimum(i + 1, n - 1),)
    smem = functools.partial(pl.BlockSpec, (tm,), memory_space=pltpu.SMEM)
    return pl.pallas_call(
        _combine_kernel,
        grid=(n,),
        in_specs=[
            smem(cur), smem(cur), smem(nxt), smem(nxt),
            pl.BlockSpec((tm, LANES), row),
            pl.BlockSpec((tm, D), row),
            pl.BlockSpec((1, D), lambda i: (0, 0)),
            pl.BlockSpec(memory_space=pl.ANY),
        ],
        out_specs=pl.BlockSpec((tm, D), row),
        out_shape=jax.ShapeDtypeStruct((N, D), F32),
        scratch_shapes=[pltpu.VMEM((2, 2, tm, D), F32), pltpu.SemaphoreType.DMA((2,))],
        compiler_params=_cparams(("arbitrary",)),
        name="moe_combine",
    )(pos0, pos1, pos0, pos1, route, x2, final_g.reshape(1, D), ys)


def _tiles(B, S):
    return dict(
        tm_in=min(512, S),
        tq=min(512, S),
        kc=min(256, S),
        tcp=min(32, S // 16),
        tm_out=min(512, S),
        tm_row=min(256, S),
        tg=min(512, S),
    )


def kernel(x, norm1_g, w_in, lambda_q1, lambda_k1, lambda_q2, lambda_k2, subln_g, conv_w, conv_b,
           lru_w_r, lru_b_r, lru_w_i, lru_b_i, lru_lambda, w_out, norm2_g, w_grp, w_exp, w_gate, w_up,
           w_down, final_g):
    B, S, D = x.shape
    t = _tiles(B, S)
    l = 0
    q, k, vt, xr, gate = _inproj(x, norm1_g[l], w_in[l], t["tm_in"])
    lam_vecs = jnp.zeros((SUBLANES, LANES), F32).at[0:4, :HEAD_DIM].set(
        jnp.stack([lambda_q1[l], lambda_k1[l], lambda_q2[l], lambda_k2[l]]))
    attn = _attention(q, k, vt, lam_vecs, subln_g[l], t["tq"], t["kc"])
    rnn = _rglru(xr, gate, conv_w[l], conv_b[l], lru_w_r[l], lru_b_r[l], lru_w_i[l], lru_b_i[l],
                 lru_lambda[l], t["tcp"])
    N = B * S
    x2, h2, route, route_t, counts = _outproj_router(
        attn.reshape(N, ATTN_WIDTH), rnn.reshape(N, LRU_WIDTH), x.reshape(N, D), w_out[l], norm2_g[l],
        w_grp[l], w_exp[l], t["tm_out"])
    tg = t["tg"]
    pos0, pos1, tile_expert, n_act, pad_start, pad_n, n_tiles = _dispatch_plan(route_t, counts, tg)
    xs = _scatter_rows(h2, pos0, pos1, pad_start, pad_n, n_act, n_tiles * tg, t["tm_row"], tg)
    ys = _experts(xs, tile_expert, n_act, w_gate[l], w_up[l], w_down[l], tg)
    out = _combine(ys, pos0, pos1, route, x2, final_g, t["tm_row"])
    return out.reshape(B, S, D)
```

```python
import functools
import math

import jax
import jax.numpy as jnp
from jax import lax
from jax.experimental import pallas as pl
from jax.experimental.pallas import tpu as pltpu

F32 = jnp.float32
BF16 = jnp.bfloat16

D_MODEL = 1024
ATTN_HEADS = 4
HEAD_DIM = 64
V_DIM = 2 * HEAD_DIM
ATTN_WIDTH = ATTN_HEADS * V_DIM
ROPE_THETA = 500000.0
ROPE_DIM = HEAD_DIM // 4
LRU_WIDTH = D_MODEL - ATTN_WIDTH
LRU_BLOCK_DIM = 64
CONV_WIDTH = 4
LRU_C = 8.0
N_GROUPS = 4
EXPERTS_PER_GROUP = 8
N_EXPERTS = N_GROUPS * EXPERTS_PER_GROUP
D_EXPERT = 256
EPS = 1e-6
SQRT_GUARD = 1e-37
LAMBDA_INIT = 0.8 - 0.6 * math.exp(0.0)

LANES = 128
SUBLANES = 8
MXU_DIM = 256
BF16_SUBLANES = 16
VT_ROWS = V_DIM + BF16_SUBLANES
VMEM_LIMIT = 56 * 1024 * 1024


def _cparams(semantics, flags=None):
    return pltpu.CompilerParams(dimension_semantics=semantics, vmem_limit_bytes=VMEM_LIMIT, flags=flags)


def _rope_tables(seq_len):
    pos = jnp.arange(seq_len, dtype=F32)
    inv_freq = ROPE_THETA ** (-jnp.arange(0, ROPE_DIM, 2, dtype=F32) / ROPE_DIM)
    ang = pos[:, None] * inv_freq[None, :]
    cos, sin = jnp.cos(ang), jnp.sin(ang)
    half = ROPE_DIM // 2
    j = jnp.arange(LANES) % HEAD_DIM
    f = j % half
    c = jnp.where(j[None, :] < ROPE_DIM, cos[:, f], 1.0)
    s_lo = jnp.where(j[None, :] < half, -sin[:, f], 0.0)
    s_hi = jnp.where((j[None, :] >= half) & (j[None, :] < ROPE_DIM), sin[:, f], 0.0)
    return c, s_lo, s_hi


def _inproj_kernel(x_ref, g_ref, w_ref, wvt_ref, c_ref, slo_ref, shi_ref,
                   q_ref, k_ref, vt_ref, xr_ref, gate_ref):
    x = x_ref[...]
    ms = jnp.mean(x * x, axis=-1, keepdims=True)
    h = (x * lax.rsqrt(ms + EPS) * g_ref[...]).astype(BF16)
    c, slo, shi = c_ref[...], slo_ref[...], shi_ref[...]

    def rope(p):
        return p * c + pltpu.roll(p, LANES - ROPE_DIM // 2, 1) * slo + pltpu.roll(p, ROPE_DIM // 2, 1) * shi

    q_scale = HEAD_DIM ** -0.5 * math.log2(math.e)
    pq = jnp.dot(h, w_ref[:, 0:ATTN_WIDTH], preferred_element_type=F32)
    pk = jnp.dot(h, w_ref[:, ATTN_WIDTH:2 * ATTN_WIDTH], preferred_element_type=F32)
    for hd in range(ATTN_HEADS):
        lo = hd * V_DIM
        q_ref[:, lo:lo + V_DIM] = (rope(pq[:, lo:lo + V_DIM]) * q_scale).astype(BF16)
        k_ref[:, lo:lo + V_DIM] = rope(pk[:, lo:lo + V_DIM]).astype(BF16)
    vt = lax.dot_general(wvt_ref[...], h, (((1,), (1,)), ((), ())), preferred_element_type=F32).astype(BF16)
    for hd in range(ATTN_HEADS):
        vt_ref[hd * VT_ROWS:hd * VT_ROWS + V_DIM, :] = vt[hd * V_DIM:(hd + 1) * V_DIM]
        vt_ref[hd * VT_ROWS + V_DIM:(hd + 1) * VT_ROWS, :] = jnp.ones((VT_ROWS - V_DIM, vt.shape[1]), BF16)
    off = 3 * ATTN_WIDTH
    xr_ref[...] = jnp.dot(h, w_ref[:, off:off + LRU_WIDTH], preferred_element_type=F32).astype(BF16)
    off += LRU_WIDTH
    gate_ref[...] = jnp.dot(h, w_ref[:, off:off + LRU_WIDTH], preferred_element_type=F32).astype(BF16)


def _inproj(x, g, w_in, tm):
    B, S, D = x.shape
    c, slo, shi = _rope_tables(S)
    row = lambda si, b: (b, si, 0)
    tab = lambda si, b: (si, 0)
    const = lambda si, b: (0, 0)
    out = jax.ShapeDtypeStruct((B, S, ATTN_WIDTH), BF16)
    w_in = w_in.astype(BF16)
    wvt = w_in[:, 2 * ATTN_WIDTH:3 * ATTN_WIDTH].T
    return pl.pallas_call(
        _inproj_kernel,
        grid=(S // tm, B),
        in_specs=[
            pl.BlockSpec((None, tm, D), row),
            pl.BlockSpec((1, D), const),
            pl.BlockSpec(w_in.shape, const),
            pl.BlockSpec(wvt.shape, const),
            pl.BlockSpec((tm, LANES), tab),
            pl.BlockSpec((tm, LANES), tab),
            pl.BlockSpec((tm, LANES), tab),
        ],
        out_specs=[pl.BlockSpec((None, tm, ATTN_WIDTH), row)] * 2
        + [pl.BlockSpec((None, ATTN_HEADS * VT_ROWS, tm), lambda si, b: (b, 0, si))]
        + [pl.BlockSpec((None, tm, LRU_WIDTH), row)] * 2,
        out_shape=[out, out, jax.ShapeDtypeStruct((B, ATTN_HEADS * VT_ROWS, S), BF16), out, out],
        compiler_params=_cparams(("arbitrary", "arbitrary")),
        name="inproj",
    )(x, g.reshape(1, D), w_in, wvt, c, slo, shi)


def _attn_kernel(lam_ref, q_ref, k_ref, vt_ref, sg_ref, o_ref, *, kc, lookahead):
    lv = lam_ref[...]
    lam = (jnp.exp(jnp.sum(lv[0:1] * lv[1:2], axis=-1, keepdims=True))
           - jnp.exp(jnp.sum(lv[2:3] * lv[3:4], axis=-1, keepdims=True)) + LAMBDA_INIT)
    q = q_ref[...]
    tq = q.shape[0]
    S = k_ref.shape[0]
    lane = lax.broadcasted_iota(jnp.int32, q.shape, 1)
    zero = jnp.zeros_like(q)
    qs = (jnp.where(lane < HEAD_DIM, q, zero), jnp.where(lane >= HEAD_DIM, q, zero))
    m = [jnp.full((1, tq), -1e30, F32) for _ in qs]
    acc = [jnp.zeros((VT_ROWS, tq), F32) for _ in qs]
    n_chunks = S // kc

    def scores(c):
        k_c = k_ref[c * kc:(c + 1) * kc, :]
        return [lax.dot_general(k_c, qm, (((1,), (1,)), ((), ())), preferred_element_type=F32) for qm in qs]

    ahead = [scores(c) for c in range(min(lookahead, n_chunks))]
    for c in range(n_chunks):
        if c + lookahead < n_chunks:
            ahead.append(scores(c + lookahead))
        vt_c = vt_ref[:, c * kc:(c + 1) * kc]
        for j, s in enumerate(ahead.pop(0)):
            m_new = jnp.maximum(m[j], jnp.max(s, axis=0, keepdims=True))
            alpha = jnp.exp2(m[j] - m_new)
            p = jnp.exp2((s - m_new).astype(BF16))
            acc[j] = alpha * acc[j] + jnp.dot(vt_c, p, preferred_element_type=F32)
            m[j] = m_new
    o = (acc[0][:V_DIM] / acc[0][V_DIM:V_DIM + 1]
         - lam * (acc[1][:V_DIM] / acc[1][V_DIM:V_DIM + 1]))
    ms = jnp.mean(o * o, axis=0, keepdims=True)
    o = (o * lax.rsqrt(ms + EPS)).T
    o_ref[...] = (o * sg_ref[...] * (1.0 - LAMBDA_INIT)).astype(o_ref.dtype)


def _attention(q, k, vt, lam_vecs, subln_g, tq, kc):
    B, S, _ = q.shape
    qmap = lambda b, h, i: (b, i, h)
    const = lambda b, h, i: (0, 0)
    return pl.pallas_call(
        functools.partial(_attn_kernel, kc=kc, lookahead=2),
        grid=(B, ATTN_HEADS, S // tq),
        in_specs=[
            pl.BlockSpec((SUBLANES, LANES), const),
            pl.BlockSpec((None, tq, V_DIM), qmap),
            pl.BlockSpec((None, S, V_DIM), lambda b, h, i: (b, 0, h)),
            pl.BlockSpec((None, VT_ROWS, S), lambda b, h, i: (b, h, 0)),
            pl.BlockSpec((1, V_DIM), const),
        ],
        out_specs=pl.BlockSpec((None, tq, V_DIM), qmap),
        out_shape=jax.ShapeDtypeStruct((B, S, ATTN_WIDTH), BF16),
        compiler_params=_cparams(("arbitrary", "arbitrary", "arbitrary")),
        name="diffattn",
    )(lam_vecs, q, k, vt, subln_g.reshape(1, V_DIM))


def _shift_rows(x, d, fill, up):
    n, c = x.shape
    if d % SUBLANES == 0:
        pad = jnp.full((d, c), fill, x.dtype)
        return jnp.concatenate([x[d:], pad]) if up else jnp.concatenate([pad, x[:n - d]])
    row = lax.broadcasted_iota(jnp.int32, (SUBLANES, c), 0)
    if up:
        r = pltpu.roll(x, n - d, 0)
        last = jnp.where(row < SUBLANES - d, r[n - SUBLANES:], fill)
        return last if n == SUBLANES else jnp.concatenate([r[:n - SUBLANES], last])
    r = pltpu.roll(x, d, 0)
    first = jnp.where(row >= d, r[:SUBLANES], fill)
    return first if n == SUBLANES else jnp.concatenate([first, r[SUBLANES:]])


def _chunk_scan(a, u, reverse):
    n = a.shape[0]
    d = 1
    while d < n:
        a_s = _shift_rows(a, d, 1.0, reverse)
        u_s = _shift_rows(u, d, 0.0, reverse)
        u = a * u_s + u
        a = a * a_s
        d *= 2
    return a, u


def _rglru_kernel(xr_ref, gate_ref, cw_ref, cb_ref, wg_ref, bg_ref, coef_ref, o_ref,
                  xp_ref, gp_ref, hf_ref, pf_ref, hb_ref, pb_ref, *, tcp):
    S, C = xr_ref.shape
    SEG = SUBLANES
    L = S // SEG
    blk = BF16_SUBLANES
    n = tcp * SEG
    halo_lo = 2 * SEG

    def permute_in(b, c):
        for j in range(SEG):
            src = pl.multiple_of(j * L + b * blk, blk)
            dst = (b * blk) * SEG + j
            xp_ref[pl.ds(halo_lo + dst, blk, stride=SEG), :] = xr_ref[pl.ds(src, blk), :].astype(F32)
            gp_ref[pl.ds(dst, blk, stride=SEG), :] = gate_ref[pl.ds(src, blk), :].astype(F32)
        return c

    lax.fori_loop(0, L // blk, permute_in, 0)
    row = lax.broadcasted_iota(jnp.int32, (SEG, C), 0)
    for k in (1, 2):
        tail = xp_ref[pl.ds(halo_lo + (L - k) * SEG, SEG), :]
        xp_ref[pl.ds(halo_lo - k * SEG, SEG), :] = jnp.where(row >= 1, pltpu.roll(tail, 1, 0), 0.0)
    head = xp_ref[pl.ds(halo_lo, SEG), :]
    xp_ref[pl.ds(halo_lo + L * SEG, SEG), :] = jnp.where(row < SEG - 1, pltpu.roll(head, SEG - 1, 0), 0.0)

    cw = cw_ref[...]
    cb = cb_ref[...]
    bg = bg_ref[...]
    coef = coef_ref[...]

    def decay_and_input(r0, direction):
        w = xp_ref[pl.ds(r0, n + 3 * SEG), :]
        xc = (w[0:n] * cw[0:1] + w[SEG:SEG + n] * cw[1:2] + w[2 * SEG:2 * SEG + n] * cw[2:3]
              + w[3 * SEG:3 * SEG + n] * cw[3:4]) + cb
        lin = jnp.dot(xc.astype(BF16), wg_ref[:, 2 * C * direction:2 * C * (direction + 1)],
                      preferred_element_type=F32)
        r = jax.nn.sigmoid(lin[:, :C] + bg[2 * direction:2 * direction + 1])
        i = jax.nn.sigmoid(lin[:, C:] + bg[2 * direction + 1:2 * direction + 2])
        log_a = coef[direction:direction + 1] * r
        th = jnp.tanh(log_a)
        one_m_a2 = (-2.0 * th) * pl.reciprocal(1.0 - th, approx=True)
        root = one_m_a2 * lax.rsqrt(jnp.maximum(one_m_a2, SQRT_GUARD))
        return jnp.exp(log_a), root * (i * xc)

    n_chunks = L // tcp

    def scan_chunk(c_idx, direction, h, p, h_ref, p_ref):
        r0 = pl.multiple_of(c_idx * n, n)
        a, u = decay_and_input(r0, direction)
        hs, ps = [None] * tcp, [None] * tcp
        for s in (range(tcp) if direction == 0 else reversed(range(tcp))):
            a_s = a[s * SEG:(s + 1) * SEG]
            h = a_s * h + u[s * SEG:(s + 1) * SEG]
            p = a_s * p
            hs[s], ps[s] = h, p
        h_ref[pl.ds(r0, n), :] = jnp.concatenate(hs)
        p_ref[pl.ds(r0, n), :] = jnp.concatenate(ps)
        return h, p

    def scan_both(ci, carry):
        hf, pf, hb, pb = carry
        hf, pf = scan_chunk(ci, 0, hf, pf, hf_ref, pf_ref)
        hb, pb = scan_chunk(n_chunks - 1 - ci, 1, hb, pb, hb_ref, pb_ref)
        return hf, pf, hb, pb

    zero, one = jnp.zeros((SEG, C), F32), jnp.ones((SEG, C), F32)
    hf_end, pf_end, hb_end, pb_end = lax.fori_loop(0, n_chunks, scan_both, (zero, one, zero, one), unroll=2)
    c_f = _shift_rows(_chunk_scan(pf_end, hf_end, False)[1], 1, 0.0, False)
    c_b = _shift_rows(_chunk_scan(pb_end, hb_end, True)[1], 1, 0.0, True)

    def finish(ci, c):
        rows = pl.ds(pl.multiple_of(ci * n, n), n)
        cf = jnp.concatenate([c_f] * tcp)
        cbk = jnp.concatenate([c_b] * tcp)
        h = hf_ref[rows, :] + pf_ref[rows, :] * cf + hb_ref[rows, :] + pb_ref[rows, :] * cbk
        gp_ref[rows, :] = h * jax.nn.gelu(gp_ref[rows, :])
        return c

    lax.fori_loop(0, L // tcp, finish, 0)

    def permute_out(b, c):
        for j in range(SEG):
            dst = pl.multiple_of(j * L + b * blk, blk)
            o_ref[pl.ds(dst, blk), :] = gp_ref[pl.ds((b * blk) * SEG + j, blk, stride=SEG), :].astype(o_ref.dtype)
        return c

    lax.fori_loop(0, L // blk, permute_out, 0)


def _block_diag(w):
    nb, d, _ = w.shape
    eye = jnp.eye(nb, dtype=w.dtype)
    return (eye[:, None, :, None] * w[:, :, None, :]).reshape(nb * d, nb * d)


def _rglru(xr, gate, conv_w, conv_b, w_r, b_r, w_i, b_i, lru_lambda, tcp):
    B, S, W = xr.shape
    C = LANES
    n_c = W // C
    mats = [_block_diag(w_r[0]), _block_diag(w_i[0]), _block_diag(w_r[1]), _block_diag(w_i[1])]
    wg = jnp.stack([jnp.concatenate([m[c * C:(c + 1) * C, c * C:(c + 1) * C] for m in mats], axis=1)
                    for c in range(n_c)]).astype(BF16)
    bg = jnp.stack([b_r[0], b_i[0], b_r[1], b_i[1]])
    coef = -LRU_C * jax.nn.softplus(-lru_lambda.astype(F32))
    blk = lambda b, c: (b, 0, c)
    par = lambda b, c: (0, c)
    return pl.pallas_call(
        functools.partial(_rglru_kernel, tcp=tcp),
        grid=(B, n_c),
        in_specs=[
            pl.BlockSpec((None, S, C), blk),
            pl.BlockSpec((None, S, C), blk),
            pl.BlockSpec((CONV_WIDTH, C), par),
            pl.BlockSpec((1, C), par),
            pl.BlockSpec((None, C, 4 * C), lambda b, c: (c, 0, 0)),
            pl.BlockSpec((4, C), par),
            pl.BlockSpec((2, C), par),
        ],
        out_specs=pl.BlockSpec((None, S, C), blk),
        out_shape=jax.ShapeDtypeStruct((B, S, W), BF16),
        scratch_shapes=[pltpu.VMEM((S + (CONV_WIDTH - 1) * SUBLANES, C), F32)] + [pltpu.VMEM((S, C), F32)] * 5,
        compiler_params=_cparams(("arbitrary", "arbitrary")),
        name="rglru",
    )(xr, gate, conv_w, conv_b.reshape(1, W), wg, bg, coef)


R_ID, R_W, R_RANK = 0, 2, 4


def _outproj_kernel(attn_ref, rnn_ref, x_ref, wa_ref, wr_ref, g_ref, wrt_ref, triu_ref,
                    x2_ref, h2_ref, route_t_ref, cnt_ref, run_ref):
    @pl.when(pl.program_id(0) == 0)
    def _():
        run_ref[...] = jnp.zeros_like(run_ref)

    mix = (jnp.dot(attn_ref[...], wa_ref[...], preferred_element_type=F32)
           + jnp.dot(rnn_ref[...], wr_ref[...], preferred_element_type=F32))
    x2 = x_ref[...] + mix
    x2_ref[...] = x2
    ms = jnp.mean(x2 * x2, axis=-1, keepdims=True)
    h2 = x2 * lax.rsqrt(ms + EPS) * g_ref[...]
    h2_ref[...] = h2
    lt = lax.dot_general(wrt_ref[...], h2.astype(BF16), (((1,), (1,)), ((), ())),
                         preferred_element_type=F32)
    tm = lt.shape[1]
    le = lt[0:N_EXPERTS]
    lg = lt[N_EXPERTS:N_EXPERTS + SUBLANES]
    neg = jnp.float32(-jnp.inf)
    big = jnp.int32(LANES)
    grow = lax.broadcasted_iota(jnp.int32, lg.shape, 0)
    is_g = grow < N_GROUPS
    gl = jnp.where(is_g, lg, neg)
    gmax = jnp.max(gl, axis=0, keepdims=True)
    gsum = jnp.sum(jnp.where(is_g, jnp.exp(gl - gmax), 0.0), axis=0, keepdims=True)
    g_top_p = 1.0 / gsum
    g_idx = jnp.min(jnp.where(is_g & (gl == gmax), grow, big), axis=0, keepdims=True)
    erow = lax.broadcasted_iota(jnp.int32, le.shape, 0)
    e_lo = g_idx * EXPERTS_PER_GROUP
    in_grp = (erow >= e_lo) & (erow < e_lo + EXPERTS_PER_GROUP)
    el = jnp.where(in_grp, le, neg)
    e1 = jnp.max(el, axis=0, keepdims=True)
    i1 = jnp.min(jnp.where(in_grp & (el == e1), erow, big), axis=0, keepdims=True)
    el2 = jnp.where(erow == i1, neg, el)
    e2 = jnp.max(el2, axis=0, keepdims=True)
    i2 = jnp.min(jnp.where(in_grp & (erow != i1) & (el2 == e2), erow, big), axis=0, keepdims=True)
    t = jnp.exp(e2 - e1)
    w1 = g_top_p / (1.0 + t)
    w2 = g_top_p * t / (1.0 + t)
    oh1 = erow == i1
    oh2 = erow == i2
    oh = jnp.where(oh1 | oh2, 1.0, 0.0)
    run = run_ref[:, 0:1]
    before = jnp.dot(oh.astype(BF16), triu_ref[...], preferred_element_type=F32) + run
    rank1 = jnp.sum(jnp.where(oh1, before, 0.0), axis=0, keepdims=True)
    rank2 = jnp.sum(jnp.where(oh2, before, 0.0), axis=0, keepdims=True)
    run_ref[...] = run_ref[...] + jnp.sum(oh, axis=1, keepdims=True)
    cnt_ref[...] = run_ref[...]
    fields = [None] * SUBLANES
    fields[R_ID], fields[R_ID + 1] = i1.astype(F32), i2.astype(F32)
    fields[R_W], fields[R_W + 1] = w1, w2
    fields[R_RANK], fields[R_RANK + 1] = rank1, rank2
    zero = jnp.zeros((1, tm), F32)
    route_t_ref[...] = jnp.concatenate([zero if f is None else f for f in fields], axis=0)


def _outproj_router(attn, rnn, x, w_out, g2, w_grp, w_exp, tm):
    N, D = x.shape
    w_out = w_out.astype(BF16)
    wrt = jnp.zeros((N_EXPERTS + SUBLANES, D), F32).at[:N_EXPERTS].set(w_exp.T)
    wrt = wrt.at[N_EXPERTS:N_EXPERTS + N_GROUPS].set(w_grp.T).astype(BF16)
    triu = jnp.triu(jnp.ones((tm, tm), BF16), 1)
    row = lambda i: (i, 0)
    const = lambda i: (0, 0)
    return pl.pallas_call(
        _outproj_kernel,
        grid=(N // tm,),
        in_specs=[
            pl.BlockSpec((tm, ATTN_WIDTH), row),
            pl.BlockSpec((tm, LRU_WIDTH), row),
            pl.BlockSpec((tm, D), row),
            pl.BlockSpec((ATTN_WIDTH, D), const),
            pl.BlockSpec((LRU_WIDTH, D), const),
            pl.BlockSpec((1, D), const),
            pl.BlockSpec(wrt.shape, const),
            pl.BlockSpec((tm, tm), const),
        ],
        out_specs=[pl.BlockSpec((tm, D), row), pl.BlockSpec((tm, D), row),
                   pl.BlockSpec((SUBLANES, tm), lambda i: (0, i)), pl.BlockSpec((N_EXPERTS, LANES), const)],
        out_shape=[jax.ShapeDtypeStruct((N, D), F32), jax.ShapeDtypeStruct((N, D), F32),
                   jax.ShapeDtypeStruct((SUBLANES, N), F32), jax.ShapeDtypeStruct((N_EXPERTS, LANES), F32)],
        scratch_shapes=[pltpu.VMEM((N_EXPERTS, LANES), F32)],
        compiler_params=_cparams(("arbitrary",)),
        name="outproj_router",
    )(attn, rnn, x, w_out[:ATTN_WIDTH], w_out[ATTN_WIDTH:], g2.reshape(1, D), wrt, triu)


def _dispatch_plan(route_t, counts, tg):
    N = route_t.shape[1]
    ids = route_t[R_ID:R_ID + 2].astype(jnp.int32)
    rank = route_t[R_RANK:R_RANK + 2].astype(jnp.int32)
    cnt = counts[:, 0].astype(jnp.int32)
    tiles_e = (cnt + tg - 1) // tg
    tile_end = jnp.cumsum(tiles_e)
    row_off = (tile_end - tiles_e) * tg
    pos = rank
    for e in range(N_EXPERTS):
        pos = pos + jnp.where(ids == e, row_off[e], 0)
    n_tiles = (2 * N) // tg + N_EXPERTS
    n_act = tile_end[-1]
    tile_id = jnp.minimum(jnp.arange(n_tiles, dtype=jnp.int32), n_act - 1)
    tile_expert = jnp.sum((tile_id[:, None] >= tile_end[None, :]).astype(jnp.int32), axis=1)
    pad_start = row_off + cnt
    pad_n = tiles_e * tg - cnt
    return pos[0], pos[1], tile_expert, n_act.reshape(1).astype(jnp.int32), pad_start, pad_n, n_tiles


def _row_copy(src, src_row, dst, dst_row, sem):
    return pltpu.make_async_copy(src.at[pl.ds(src_row, 1)], dst.at[pl.ds(dst_row, 1)], sem)


def _scatter_kernel(pad_start_ref, pad_n_ref, nact_ref, pos0_ref, pos1_ref, h_ref, xs_ref,
                    zero_ref, stage_ref, sem, sems, *, tg):
    tm = h_ref.shape[0]
    half = tg // 2

    @pl.when(pl.program_id(0) == 0)
    def _():
        zero_ref[...] = jnp.zeros_like(zero_ref)

        def fill_tile(j, c):
            base = pl.multiple_of(j * tg, tg)
            cps = [pltpu.make_async_copy(zero_ref, xs_ref.at[pl.ds(base + h * half, half)], sem) for h in range(2)]
            for cp in cps:
                cp.start()
            for cp in cps:
                cp.wait()
            return c

        lax.fori_loop(nact_ref[0], xs_ref.shape[0] // tg, fill_tile, 0)

    @pl.when(pl.program_id(0) == 0)
    def _():
        for e in range(N_EXPERTS):
            n = pad_n_ref[e]
            off = pad_start_ref[e]
            b = 1
            while b < tg:
                hit = (n & b) != 0

                @pl.when(hit)
                def _(off=off, b=b):
                    if b < SUBLANES:
                        cps = [_row_copy(zero_ref, 0, xs_ref, off + i, sem) for i in range(b)]
                    else:
                        cps = [pltpu.make_async_copy(zero_ref.at[pl.ds(0, b)],
                                                     xs_ref.at[pl.ds(pl.multiple_of(off, SUBLANES), b)], sem)]
                    for cp in cps:
                        cp.start()
                    for cp in cps:
                        cp.wait()

                off = off + jnp.where(hit, b, 0)
                b *= 2

    step = pl.program_id(0)
    slot = step % 2
    stage = stage_ref.at[slot]
    stage[...] = h_ref[...]

    def issue(r, c):
        _row_copy(stage, r, xs_ref, pos0_ref[r], sems.at[slot]).start(priority=0)
        _row_copy(stage, r, xs_ref, pos1_ref[r], sems.at[slot]).start(priority=1)
        return c

    lax.fori_loop(0, tm, issue, 0, unroll=8)

    def drain(which):
        def body(r, c):
            _row_copy(stage, 0, xs_ref, 0, sems.at[which]).wait()
            _row_copy(stage, 0, xs_ref, 0, sems.at[which]).wait()
            return c

        lax.fori_loop(0, tm, body, 0, unroll=8)

    @pl.when(step > 0)
    def _():
        drain(1 - slot)

    @pl.when(step == pl.num_programs(0) - 1)
    def _():
        drain(slot)


def _scatter_rows(h2, pos0, pos1, pad_start, pad_n, n_act, n_rows, tm, tg):
    N, D = h2.shape
    tok = lambda i, ps, pn, na: (i,)
    return pl.pallas_call(
        functools.partial(_scatter_kernel, tg=tg),
        grid_spec=pltpu.PrefetchScalarGridSpec(
            num_scalar_prefetch=3,
            grid=(N // tm,),
            in_specs=[
                pl.BlockSpec((tm,), tok, memory_space=pltpu.SMEM),
                pl.BlockSpec((tm,), tok, memory_space=pltpu.SMEM),
                pl.BlockSpec((tm, D), lambda i, ps, pn, na: (i, 0)),
            ],
            out_specs=pl.BlockSpec(memory_space=pl.ANY),
            scratch_shapes=[pltpu.VMEM((tg // 2, D), F32), pltpu.VMEM((2, tm, D), F32),
                            pltpu.SemaphoreType.DMA, pltpu.SemaphoreType.DMA((2,))],
        ),
        out_shape=jax.ShapeDtypeStruct((n_rows, D), F32),
        compiler_params=_cparams(("arbitrary",)),
        name="moe_scatter",
    )(pad_start, pad_n, n_act, pos0, pos1, h2)


def _experts_kernel(te_ref, nact_ref, x_ref, wg_ref, wu_ref, wd_ref, y_ref, wg_s, wu_s, wd_s):
    j = pl.program_id(0)
    active = j < nact_ref[0]
    new_expert = (j == 0) | (te_ref[j] != te_ref[jnp.maximum(j - 1, 0)])

    @pl.when(active & new_expert)
    def _():
        wg_s[...] = wg_ref[...].astype(BF16)
        wu_s[...] = wu_ref[...].astype(BF16)
        wd_s[...] = wd_ref[...].astype(BF16)

    @pl.when(active)
    def _():
        x = x_ref[...].astype(BF16)
        gt = jnp.dot(x, wg_s[...], preferred_element_type=F32)
        up = jnp.dot(x, wu_s[...], preferred_element_type=F32)
        hid = (gt * jax.nn.sigmoid(gt)) * up
        y_ref[...] = jnp.dot(hid.astype(BF16), wd_s[...], preferred_element_type=F32)

    @pl.when(jnp.logical_not(active))
    def _():
        y_ref[...] = jnp.zeros_like(y_ref)


def _experts(xs, tile_expert, n_act, w_gate, w_up, w_down, tg):
    R, D = xs.shape
    rows = lambda j, te, na: (j, 0)
    wmap = lambda j, te, na: (te[j], 0, 0)
    return pl.pallas_call(
        _experts_kernel,
        grid_spec=pltpu.PrefetchScalarGridSpec(
            num_scalar_prefetch=2,
            grid=(R // tg,),
            in_specs=[
                pl.BlockSpec((tg, D), rows),
                pl.BlockSpec((None, D, D_EXPERT), wmap),
                pl.BlockSpec((None, D, D_EXPERT), wmap),
                pl.BlockSpec((None, D_EXPERT, D), wmap),
            ],
            out_specs=pl.BlockSpec((tg, D), rows),
            scratch_shapes=[pltpu.VMEM((D, D_EXPERT), BF16), pltpu.VMEM((D, D_EXPERT), BF16),
                            pltpu.VMEM((D_EXPERT, D), BF16)],
        ),
        out_shape=jax.ShapeDtypeStruct((R, D), F32),
        compiler_params=_cparams(("arbitrary",)),
        name="moe_experts",
    )(tile_expert, n_act, xs, w_gate, w_up, w_down)


def _combine_kernel(pos0_ref, pos1_ref, pos0n_ref, pos1n_ref, route_ref, x2_ref, fg_ref, ys_ref, o_ref,
                    buf_ref, sems):
    tm = x2_ref.shape[0]
    step = pl.program_id(0)
    slot = step % 2

    def issue(p0_ref, p1_ref, which):
        def body(r, c):
            _row_copy(ys_ref, p0_ref[r], buf_ref.at[which, 0], r, sems.at[which]).start(priority=0)
            _row_copy(ys_ref, p1_ref[r], buf_ref.at[which, 1], r, sems.at[which]).start(priority=1)
            return c

        lax.fori_loop(0, tm, body, 0, unroll=8)

    @pl.when(step == 0)
    def _():
        issue(pos0_ref, pos1_ref, slot)

    @pl.when(step + 1 < pl.num_programs(0))
    def _():
        issue(pos0n_ref, pos1n_ref, 1 - slot)

    def drain(r, c):
        _row_copy(ys_ref, 0, buf_ref.at[slot, 0], 0, sems.at[slot]).wait()
        _row_copy(ys_ref, 0, buf_ref.at[slot, 0], 0, sems.at[slot]).wait()
        return c

    lax.fori_loop(0, tm, drain, 0, unroll=8)
    route = route_ref[...].T
    w1 = route[:, R_W:R_W + 1]
    w2 = route[:, R_W + 1:R_W + 2]
    y = x2_ref[...] + w1 * buf_ref[slot, 0] + w2 * buf_ref[slot, 1]
    ms = jnp.mean(y * y, axis=-1, keepdims=True)
    o_ref[...] = y * lax.rsqrt(ms + EPS) * fg_ref[...]


def _combine(ys, pos0, pos1, route, x2, final_g, tm):
    N, D = x2.shape
    n = N // tm
    row = lambda i: (i, 0)
    cur = lambda i: (i,)
    nxt = lambda i: (jnp.minimum(i + 1, n - 1),)
    smem = functools.partial(pl.BlockSpec, (tm,), memory_space=pltpu.SMEM)
    return pl.pallas_call(
        _combine_kernel,
        grid=(n,),
        in_specs=[
            smem(cur), smem(cur), smem(nxt), smem(nxt),
            pl.BlockSpec((SUBLANES, tm), lambda i: (0, i)),
            pl.BlockSpec((tm, D), row),
            pl.BlockSpec((1, D), lambda i: (0, 0)),
            pl.BlockSpec(memory_space=pl.ANY),
        ],
        out_specs=pl.BlockSpec((tm, D), row),
        out_shape=jax.ShapeDtypeStruct((N, D), F32),
        scratch_shapes=[pltpu.VMEM((2, 2, tm, D), F32), pltpu.SemaphoreType.DMA((2,))],
        compiler_params=_cparams(("arbitrary",)),
        name="moe_combine",
    )(pos0, pos1, pos0, pos1, route, x2, final_g.reshape(1, D), ys)


def _tiles(B, S):
    return dict(
        tm_in=min(512, S),
        tq=min(512, S),
        kc=min(256, S),
        tcp=min(32, S // 16),
        tm_out=min(512, S),
        tm_row=min(512, S),
        tg=min(512, S),
    )


def kernel(x, norm1_g, w_in, lambda_q1, lambda_k1, lambda_q2, lambda_k2, subln_g, conv_w, conv_b,
           lru_w_r, lru_b_r, lru_w_i, lru_b_i, lru_lambda, w_out, norm2_g, w_grp, w_exp, w_gate, w_up,
           w_down, final_g):
    B, S, D = x.shape
    t = _tiles(B, S)
    l = 0
    q, k, vt, xr, gate = _inproj(x, norm1_g[l], w_in[l], t["tm_in"])
    lam_vecs = jnp.zeros((SUBLANES, LANES), F32).at[0:4, :HEAD_DIM].set(
        jnp.stack([lambda_q1[l], lambda_k1[l], lambda_q2[l], lambda_k2[l]]))
    attn = _attention(q, k, vt, lam_vecs, subln_g[l], t["tq"], t["kc"])
    rnn = _rglru(xr, gate, conv_w[l], conv_b[l], lru_w_r[l], lru_b_r[l], lru_w_i[l], lru_b_i[l],
                 lru_lambda[l], t["tcp"])
    N = B * S
    x2, h2, route_t, counts = _outproj_router(
        attn.reshape(N, ATTN_WIDTH), rnn.reshape(N, LRU_WIDTH), x.reshape(N, D), w_out[l], norm2_g[l],
        w_grp[l], w_exp[l], t["tm_out"])
    tg = t["tg"]
    pos0, pos1, tile_expert, n_act, pad_start, pad_n, n_tiles = _dispatch_plan(route_t, counts, tg)
    xs = _scatter_rows(h2, pos0, pos1, pad_start, pad_n, n_act, n_tiles * tg, t["tm_row"], tg)
    ys = _experts(xs, tile_expert, n_act, w_gate[l], w_up[l], w_down[l], tg)
    out = _combine(ys, pos0, pos1, route_t, x2, final_g, t["tm_row"])
    return out.reshape(B, S, D)
```

```python
import functools
import math

import jax
import jax.numpy as jnp
from jax import lax
from jax.experimental import pallas as pl
from jax.experimental.pallas import tpu as pltpu

F32 = jnp.float32
BF16 = jnp.bfloat16

D_MODEL = 1024
ATTN_HEADS = 4
HEAD_DIM = 64
V_DIM = 2 * HEAD_DIM
ATTN_WIDTH = ATTN_HEADS * V_DIM
ROPE_THETA = 500000.0
ROPE_DIM = HEAD_DIM // 4
LRU_WIDTH = D_MODEL - ATTN_WIDTH
LRU_BLOCK_DIM = 64
CONV_WIDTH = 4
LRU_C = 8.0
N_GROUPS = 4
EXPERTS_PER_GROUP = 8
N_EXPERTS = N_GROUPS * EXPERTS_PER_GROUP
D_EXPERT = 256
EPS = 1e-6
SQRT_GUARD = 1e-37
LAMBDA_INIT = 0.8 - 0.6 * math.exp(0.0)

LANES = 128
SUBLANES = 8
MXU_DIM = 256
BF16_SUBLANES = 16
VT_ROWS = V_DIM + BF16_SUBLANES
ROW_SUB = D_MODEL // LANES
assert ROW_SUB == SUBLANES
VMEM_LIMIT = 56 * 1024 * 1024


def _cparams(semantics, flags=None):
    return pltpu.CompilerParams(dimension_semantics=semantics, vmem_limit_bytes=VMEM_LIMIT, flags=flags)


def _rope_tables(seq_len):
    pos = jnp.arange(seq_len, dtype=F32)
    inv_freq = ROPE_THETA ** (-jnp.arange(0, ROPE_DIM, 2, dtype=F32) / ROPE_DIM)
    ang = pos[:, None] * inv_freq[None, :]
    cos, sin = jnp.cos(ang), jnp.sin(ang)
    half = ROPE_DIM // 2
    j = jnp.arange(LANES) % HEAD_DIM
    f = j % half
    c = jnp.where(j[None, :] < ROPE_DIM, cos[:, f], 1.0)
    s_lo = jnp.where(j[None, :] < half, -sin[:, f], 0.0)
    s_hi = jnp.where((j[None, :] >= half) & (j[None, :] < ROPE_DIM), sin[:, f], 0.0)
    return c, s_lo, s_hi


def _inproj_kernel(x_ref, g_ref, w_ref, wvt_ref, c_ref, slo_ref, shi_ref,
                   q_ref, k_ref, vt_ref, xr_ref, gate_ref):
    x = x_ref[...]
    ms = jnp.mean(x * x, axis=-1, keepdims=True)
    h = (x * lax.rsqrt(ms + EPS) * g_ref[...]).astype(BF16)
    c, slo, shi = c_ref[...], slo_ref[...], shi_ref[...]

    def rope(p):
        return p * c + pltpu.roll(p, LANES - ROPE_DIM // 2, 1) * slo + pltpu.roll(p, ROPE_DIM // 2, 1) * shi

    q_scale = HEAD_DIM ** -0.5 * math.log2(math.e)
    pq = jnp.dot(h, w_ref[:, 0:ATTN_WIDTH], preferred_element_type=F32)
    pk = jnp.dot(h, w_ref[:, ATTN_WIDTH:2 * ATTN_WIDTH], preferred_element_type=F32)
    for hd in range(ATTN_HEADS):
        lo = hd * V_DIM
        q_ref[:, lo:lo + V_DIM] = (rope(pq[:, lo:lo + V_DIM]) * q_scale).astype(BF16)
        k_ref[:, lo:lo + V_DIM] = rope(pk[:, lo:lo + V_DIM]).astype(BF16)
    vt = lax.dot_general(wvt_ref[...], h, (((1,), (1,)), ((), ())), preferred_element_type=F32).astype(BF16)
    for hd in range(ATTN_HEADS):
        vt_ref[hd * VT_ROWS:hd * VT_ROWS + V_DIM, :] = vt[hd * V_DIM:(hd + 1) * V_DIM]
        vt_ref[hd * VT_ROWS + V_DIM:(hd + 1) * VT_ROWS, :] = jnp.ones((VT_ROWS - V_DIM, vt.shape[1]), BF16)
    off = 3 * ATTN_WIDTH
    xr_ref[...] = jnp.dot(h, w_ref[:, off:off + LRU_WIDTH], preferred_element_type=F32).astype(BF16)
    off += LRU_WIDTH
    gate_ref[...] = jnp.dot(h, w_ref[:, off:off + LRU_WIDTH], preferred_element_type=F32).astype(BF16)


def _inproj(x, g, w_in, tm):
    B, S, D = x.shape
    c, slo, shi = _rope_tables(S)
    row = lambda si, b: (b, si, 0)
    tab = lambda si, b: (si, 0)
    const = lambda si, b: (0, 0)
    out = jax.ShapeDtypeStruct((B, S, ATTN_WIDTH), BF16)
    w_in = w_in.astype(BF16)
    wvt = w_in[:, 2 * ATTN_WIDTH:3 * ATTN_WIDTH].T
    return pl.pallas_call(
        _inproj_kernel,
        grid=(S // tm, B),
        in_specs=[
            pl.BlockSpec((None, tm, D), row),
            pl.BlockSpec((1, D), const),
            pl.BlockSpec(w_in.shape, const),
            pl.BlockSpec(wvt.shape, const),
            pl.BlockSpec((tm, LANES), tab),
            pl.BlockSpec((tm, LANES), tab),
            pl.BlockSpec((tm, LANES), tab),
        ],
        out_specs=[pl.BlockSpec((None, tm, ATTN_WIDTH), row)] * 2
        + [pl.BlockSpec((None, ATTN_HEADS * VT_ROWS, tm), lambda si, b: (b, 0, si))]
        + [pl.BlockSpec((None, tm, LRU_WIDTH), row)] * 2,
        out_shape=[out, out, jax.ShapeDtypeStruct((B, ATTN_HEADS * VT_ROWS, S), BF16), out, out],
        compiler_params=_cparams(("arbitrary", "arbitrary")),
        name="inproj",
    )(x, g.reshape(1, D), w_in, wvt, c, slo, shi)


def _attn_kernel(lam_ref, q_ref, k_ref, vt_ref, sg_ref, o_ref, *, kc, lookahead):
    lv = lam_ref[...]
    lam = (jnp.exp(jnp.sum(lv[0:1] * lv[1:2], axis=-1, keepdims=True))
           - jnp.exp(jnp.sum(lv[2:3] * lv[3:4], axis=-1, keepdims=True)) + LAMBDA_INIT)
    q = q_ref[...]
    tq = q.shape[0]
    S = k_ref.shape[0]
    lane = lax.broadcasted_iota(jnp.int32, q.shape, 1)
    zero = jnp.zeros_like(q)
    qs = (jnp.where(lane < HEAD_DIM, q, zero), jnp.where(lane >= HEAD_DIM, q, zero))
    m = [jnp.full((1, tq), -1e30, F32) for _ in qs]
    acc = [jnp.zeros((VT_ROWS, tq), F32) for _ in qs]
    n_chunks = S // kc

    def scores(c):
        k_c = k_ref[c * kc:(c + 1) * kc, :]
        return [lax.dot_general(k_c, qm, (((1,), (1,)), ((), ())), preferred_element_type=F32) for qm in qs]

    ahead = [scores(c) for c in range(min(lookahead, n_chunks))]
    for c in range(n_chunks):
        if c + lookahead < n_chunks:
            ahead.append(scores(c + lookahead))
        vt_c = vt_ref[:, c * kc:(c + 1) * kc]
        for j, s in enumerate(ahead.pop(0)):
            m_new = jnp.maximum(m[j], jnp.max(s, axis=0, keepdims=True))
            alpha = jnp.exp2(m[j] - m_new)
            p = jnp.exp2((s - m_new).astype(BF16))
            acc[j] = alpha * acc[j] + jnp.dot(vt_c, p, preferred_element_type=F32)
            m[j] = m_new
    o = (acc[0][:V_DIM] / acc[0][V_DIM:V_DIM + 1]
         - lam * (acc[1][:V_DIM] / acc[1][V_DIM:V_DIM + 1]))
    ms = jnp.mean(o * o, axis=0, keepdims=True)
    o = (o * lax.rsqrt(ms + EPS)).T
    o_ref[...] = (o * sg_ref[...] * (1.0 - LAMBDA_INIT)).astype(o_ref.dtype)


def _attention(q, k, vt, lam_vecs, subln_g, tq, kc):
    B, S, _ = q.shape
    qmap = lambda b, h, i: (b, i, h)
    const = lambda b, h, i: (0, 0)
    return pl.pallas_call(
        functools.partial(_attn_kernel, kc=kc, lookahead=2),
        grid=(B, ATTN_HEADS, S // tq),
        in_specs=[
            pl.BlockSpec((SUBLANES, LANES), const),
            pl.BlockSpec((None, tq, V_DIM), qmap),
            pl.BlockSpec((None, S, V_DIM), lambda b, h, i: (b, 0, h)),
            pl.BlockSpec((None, VT_ROWS, S), lambda b, h, i: (b, h, 0)),
            pl.BlockSpec((1, V_DIM), const),
        ],
        out_specs=pl.BlockSpec((None, tq, V_DIM), qmap),
        out_shape=jax.ShapeDtypeStruct((B, S, ATTN_WIDTH), BF16),
        compiler_params=_cparams(("arbitrary", "arbitrary", "arbitrary")),
        name="diffattn",
    )(lam_vecs, q, k, vt, subln_g.reshape(1, V_DIM))


def _shift_rows(x, d, fill, up):
    n, c = x.shape
    if d % SUBLANES == 0:
        pad = jnp.full((d, c), fill, x.dtype)
        return jnp.concatenate([x[d:], pad]) if up else jnp.concatenate([pad, x[:n - d]])
    row = lax.broadcasted_iota(jnp.int32, (SUBLANES, c), 0)
    if up:
        r = pltpu.roll(x, n - d, 0)
        last = jnp.where(row < SUBLANES - d, r[n - SUBLANES:], fill)
        return last if n == SUBLANES else jnp.concatenate([r[:n - SUBLANES], last])
    r = pltpu.roll(x, d, 0)
    first = jnp.where(row >= d, r[:SUBLANES], fill)
    return first if n == SUBLANES else jnp.concatenate([first, r[SUBLANES:]])


def _chunk_scan(a, u, reverse):
    n = a.shape[0]
    d = 1
    while d < n:
        a_s = _shift_rows(a, d, 1.0, reverse)
        u_s = _shift_rows(u, d, 0.0, reverse)
        u = a * u_s + u
        a = a * a_s
        d *= 2
    return a, u


def _rglru_kernel(xr_ref, gate_ref, cw_ref, cb_ref, wg_ref, bg_ref, coef_ref, o_ref,
                  xp_ref, gp_ref, hf_ref, pf_ref, hb_ref, pb_ref, *, tcp):
    S, C = xr_ref.shape
    SEG = SUBLANES
    L = S // SEG
    blk = BF16_SUBLANES
    n = tcp * SEG
    halo_lo = 2 * SEG

    def permute_in(b, c):
        for j in range(SEG):
            src = pl.multiple_of(j * L + b * blk, blk)
            dst = (b * blk) * SEG + j
            xp_ref[pl.ds(halo_lo + dst, blk, stride=SEG), :] = xr_ref[pl.ds(src, blk), :].astype(F32)
            gp_ref[pl.ds(dst, blk, stride=SEG), :] = gate_ref[pl.ds(src, blk), :].astype(F32)
        return c

    lax.fori_loop(0, L // blk, permute_in, 0)
    row = lax.broadcasted_iota(jnp.int32, (SEG, C), 0)
    for k in (1, 2):
        tail = xp_ref[pl.ds(halo_lo + (L - k) * SEG, SEG), :]
        xp_ref[pl.ds(halo_lo - k * SEG, SEG), :] = jnp.where(row >= 1, pltpu.roll(tail, 1, 0), 0.0)
    head = xp_ref[pl.ds(halo_lo, SEG), :]
    xp_ref[pl.ds(halo_lo + L * SEG, SEG), :] = jnp.where(row < SEG - 1, pltpu.roll(head, SEG - 1, 0), 0.0)

    cw = cw_ref[...]
    cb = cb_ref[...]
    bg = bg_ref[...]
    coef = coef_ref[...]

    def decay_and_input(r0, direction):
        w = xp_ref[pl.ds(r0, n + 3 * SEG), :]
        xc = (w[0:n] * cw[0:1] + w[SEG:SEG + n] * cw[1:2] + w[2 * SEG:2 * SEG + n] * cw[2:3]
              + w[3 * SEG:3 * SEG + n] * cw[3:4]) + cb
        lin = jnp.dot(xc.astype(BF16), wg_ref[:, 2 * C * direction:2 * C * (direction + 1)],
                      preferred_element_type=F32)
        r = jax.nn.sigmoid(lin[:, :C] + bg[2 * direction:2 * direction + 1])
        i = jax.nn.sigmoid(lin[:, C:] + bg[2 * direction + 1:2 * direction + 2])
        log_a = coef[direction:direction + 1] * r
        th = jnp.tanh(log_a)
        one_m_a2 = (-2.0 * th) * pl.reciprocal(1.0 - th, approx=True)
        root = one_m_a2 * lax.rsqrt(jnp.maximum(one_m_a2, SQRT_GUARD))
        return jnp.exp(log_a), root * (i * xc)

    n_chunks = L // tcp

    def scan_chunk(c_idx, direction, h, p, h_ref, p_ref):
        r0 = pl.multiple_of(c_idx * n, n)
        a, u = decay_and_input(r0, direction)
        hs, ps = [None] * tcp, [None] * tcp
        for s in (range(tcp) if direction == 0 else reversed(range(tcp))):
            a_s = a[s * SEG:(s + 1) * SEG]
            h = a_s * h + u[s * SEG:(s + 1) * SEG]
            p = a_s * p
            hs[s], ps[s] = h, p
        h_ref[pl.ds(r0, n), :] = jnp.concatenate(hs)
        p_ref[pl.ds(r0, n), :] = jnp.concatenate(ps)
        return h, p

    def scan_both(ci, carry):
        hf, pf, hb, pb = carry
        hf, pf = scan_chunk(ci, 0, hf, pf, hf_ref, pf_ref)
        hb, pb = scan_chunk(n_chunks - 1 - ci, 1, hb, pb, hb_ref, pb_ref)
        return hf, pf, hb, pb

    zero, one = jnp.zeros((SEG, C), F32), jnp.ones((SEG, C), F32)
    hf_end, pf_end, hb_end, pb_end = lax.fori_loop(0, n_chunks, scan_both, (zero, one, zero, one), unroll=2)
    c_f = _shift_rows(_chunk_scan(pf_end, hf_end, False)[1], 1, 0.0, False)
    c_b = _shift_rows(_chunk_scan(pb_end, hb_end, True)[1], 1, 0.0, True)

    def finish(ci, c):
        rows = pl.ds(pl.multiple_of(ci * n, n), n)
        cf = jnp.concatenate([c_f] * tcp)
        cbk = jnp.concatenate([c_b] * tcp)
        h = hf_ref[rows, :] + pf_ref[rows, :] * cf + hb_ref[rows, :] + pb_ref[rows, :] * cbk
        gp_ref[rows, :] = h * jax.nn.gelu(gp_ref[rows, :])
        return c

    lax.fori_loop(0, L // tcp, finish, 0)

    def permute_out(b, c):
        for j in range(SEG):
            dst = pl.multiple_of(j * L + b * blk, blk)
            o_ref[pl.ds(dst, blk), :] = gp_ref[pl.ds((b * blk) * SEG + j, blk, stride=SEG), :].astype(o_ref.dtype)
        return c

    lax.fori_loop(0, L // blk, permute_out, 0)


def _block_diag(w):
    nb, d, _ = w.shape
    eye = jnp.eye(nb, dtype=w.dtype)
    return (eye[:, None, :, None] * w[:, :, None, :]).reshape(nb * d, nb * d)


def _rglru(xr, gate, conv_w, conv_b, w_r, b_r, w_i, b_i, lru_lambda, tcp):
    B, S, W = xr.shape
    C = LANES
    n_c = W // C
    mats = [_block_diag(w_r[0]), _block_diag(w_i[0]), _block_diag(w_r[1]), _block_diag(w_i[1])]
    wg = jnp.stack([jnp.concatenate([m[c * C:(c + 1) * C, c * C:(c + 1) * C] for m in mats], axis=1)
                    for c in range(n_c)]).astype(BF16)
    bg = jnp.stack([b_r[0], b_i[0], b_r[1], b_i[1]])
    coef = -LRU_C * jax.nn.softplus(-lru_lambda.astype(F32))
    blk = lambda b, c: (b, 0, c)
    par = lambda b, c: (0, c)
    return pl.pallas_call(
        functools.partial(_rglru_kernel, tcp=tcp),
        grid=(B, n_c),
        in_specs=[
            pl.BlockSpec((None, S, C), blk),
            pl.BlockSpec((None, S, C), blk),
            pl.BlockSpec((CONV_WIDTH, C), par),
            pl.BlockSpec((1, C), par),
            pl.BlockSpec((None, C, 4 * C), lambda b, c: (c, 0, 0)),
            pl.BlockSpec((4, C), par),
            pl.BlockSpec((2, C), par),
        ],
        out_specs=pl.BlockSpec((None, S, C), blk),
        out_shape=jax.ShapeDtypeStruct((B, S, W), BF16),
        scratch_shapes=[pltpu.VMEM((S + (CONV_WIDTH - 1) * SUBLANES, C), F32)] + [pltpu.VMEM((S, C), F32)] * 5,
        compiler_params=_cparams(("arbitrary", "arbitrary")),
        name="rglru",
    )(xr, gate, conv_w, conv_b.reshape(1, W), wg, bg, coef)


def _to_tiles(ref, val):
    rows = val.shape[0]
    for j in range(ROW_SUB):
        ref[pl.ds(j, rows, stride=ROW_SUB), :] = val[:, j * LANES:(j + 1) * LANES]


def _from_tiles(ref, rows):
    return jnp.concatenate([ref[pl.ds(j, rows, stride=ROW_SUB), :] for j in range(ROW_SUB)], axis=1)


def _tile_rows(r, n=1):
    start = r * ROW_SUB if isinstance(r, int) else pl.multiple_of(r * ROW_SUB, ROW_SUB)
    return pl.ds(start, n * ROW_SUB)


R_ID, R_W, R_RANK = 0, 2, 4


def _outproj_kernel(attn_ref, rnn_ref, x_ref, wa_ref, wr_ref, g_ref, wrt_ref, triu_ref,
                    x2_ref, h2_ref, route_t_ref, cnt_ref, run_ref):
    @pl.when(pl.program_id(0) == 0)
    def _():
        run_ref[...] = jnp.zeros_like(run_ref)

    mix = (jnp.dot(attn_ref[...], wa_ref[...], preferred_element_type=F32)
           + jnp.dot(rnn_ref[...], wr_ref[...], preferred_element_type=F32))
    x2 = x_ref[...] + mix
    x2_ref[...] = x2
    ms = jnp.mean(x2 * x2, axis=-1, keepdims=True)
    h2 = x2 * lax.rsqrt(ms + EPS) * g_ref[...]
    _to_tiles(h2_ref, h2)
    lt = lax.dot_general(wrt_ref[...], h2.astype(BF16), (((1,), (1,)), ((), ())),
                         preferred_element_type=F32)
    tm = lt.shape[1]
    le = lt[0:N_EXPERTS]
    lg = lt[N_EXPERTS:N_EXPERTS + SUBLANES]
    neg = jnp.float32(-jnp.inf)
    big = jnp.int32(LANES)
    grow = lax.broadcasted_iota(jnp.int32, lg.shape, 0)
    is_g = grow < N_GROUPS
    gl = jnp.where(is_g, lg, neg)
    gmax = jnp.max(gl, axis=0, keepdims=True)
    gsum = jnp.sum(jnp.where(is_g, jnp.exp(gl - gmax), 0.0), axis=0, keepdims=True)
    g_top_p = 1.0 / gsum
    g_idx = jnp.min(jnp.where(is_g & (gl == gmax), grow, big), axis=0, keepdims=True)
    erow = lax.broadcasted_iota(jnp.int32, le.shape, 0)
    e_lo = g_idx * EXPERTS_PER_GROUP
    in_grp = (erow >= e_lo) & (erow < e_lo + EXPERTS_PER_GROUP)
    el = jnp.where(in_grp, le, neg)
    e1 = jnp.max(el, axis=0, keepdims=True)
    i1 = jnp.min(jnp.where(in_grp & (el == e1), erow, big), axis=0, keepdims=True)
    el2 = jnp.where(erow == i1, neg, el)
    e2 = jnp.max(el2, axis=0, keepdims=True)
    i2 = jnp.min(jnp.where(in_grp & (erow != i1) & (el2 == e2), erow, big), axis=0, keepdims=True)
    t = jnp.exp(e2 - e1)
    w1 = g_top_p / (1.0 + t)
    w2 = g_top_p * t / (1.0 + t)
    oh1 = erow == i1
    oh2 = erow == i2
    oh = jnp.where(oh1 | oh2, 1.0, 0.0)
    run = run_ref[:, 0:1]
    before = jnp.dot(oh.astype(BF16), triu_ref[...], preferred_element_type=F32) + run
    rank1 = jnp.sum(jnp.where(oh1, before, 0.0), axis=0, keepdims=True)
    rank2 = jnp.sum(jnp.where(oh2, before, 0.0), axis=0, keepdims=True)
    run_ref[...] = run_ref[...] + jnp.sum(oh, axis=1, keepdims=True)
    cnt_ref[...] = run_ref[...]
    fields = [None] * SUBLANES
    fields[R_ID], fields[R_ID + 1] = i1.astype(F32), i2.astype(F32)
    fields[R_W], fields[R_W + 1] = w1, w2
    fields[R_RANK], fields[R_RANK + 1] = rank1, rank2
    zero = jnp.zeros((1, tm), F32)
    route_t_ref[...] = jnp.concatenate([zero if f is None else f for f in fields], axis=0)


def _outproj_router(attn, rnn, x, w_out, g2, w_grp, w_exp, tm):
    N, D = x.shape
    w_out = w_out.astype(BF16)
    wrt = jnp.zeros((N_EXPERTS + SUBLANES, D), F32).at[:N_EXPERTS].set(w_exp.T)
    wrt = wrt.at[N_EXPERTS:N_EXPERTS + N_GROUPS].set(w_grp.T).astype(BF16)
    triu = jnp.triu(jnp.ones((tm, tm), BF16), 1)
    row = lambda i: (i, 0)
    const = lambda i: (0, 0)
    return pl.pallas_call(
        _outproj_kernel,
        grid=(N // tm,),
        in_specs=[
            pl.BlockSpec((tm, ATTN_WIDTH), row),
            pl.BlockSpec((tm, LRU_WIDTH), row),
            pl.BlockSpec((tm, D), row),
            pl.BlockSpec((ATTN_WIDTH, D), const),
            pl.BlockSpec((LRU_WIDTH, D), const),
            pl.BlockSpec((1, D), const),
            pl.BlockSpec(wrt.shape, const),
            pl.BlockSpec((tm, tm), const),
        ],
        out_specs=[pl.BlockSpec((tm, D), row), pl.BlockSpec((tm * ROW_SUB, LANES), row),
                   pl.BlockSpec((SUBLANES, tm), lambda i: (0, i)), pl.BlockSpec((N_EXPERTS, LANES), const)],
        out_shape=[jax.ShapeDtypeStruct((N, D), F32), jax.ShapeDtypeStruct((N * ROW_SUB, LANES), F32),
                   jax.ShapeDtypeStruct((SUBLANES, N), F32), jax.ShapeDtypeStruct((N_EXPERTS, LANES), F32)],
        scratch_shapes=[pltpu.VMEM((N_EXPERTS, LANES), F32)],
        compiler_params=_cparams(("arbitrary",)),
        name="outproj_router",
    )(attn, rnn, x, w_out[:ATTN_WIDTH], w_out[ATTN_WIDTH:], g2.reshape(1, D), wrt, triu)


def _dispatch_plan(route_t, counts, tg):
    N = route_t.shape[1]
    ids = route_t[R_ID:R_ID + 2].astype(jnp.int32)
    rank = route_t[R_RANK:R_RANK + 2].astype(jnp.int32)
    cnt = counts[:, 0].astype(jnp.int32)
    tiles_e = (cnt + tg - 1) // tg
    tile_end = jnp.cumsum(tiles_e)
    row_off = (tile_end - tiles_e) * tg
    pos = rank
    for e in range(N_EXPERTS):
        pos = pos + jnp.where(ids == e, row_off[e], 0)
    n_tiles = (2 * N) // tg + N_EXPERTS
    n_act = tile_end[-1]
    tile_id = jnp.minimum(jnp.arange(n_tiles, dtype=jnp.int32), n_act - 1)
    tile_expert = jnp.sum((tile_id[:, None] >= tile_end[None, :]).astype(jnp.int32), axis=1)
    pad_start = row_off + cnt
    pad_n = tiles_e * tg - cnt
    return pos[0], pos[1], tile_expert, n_act.reshape(1).astype(jnp.int32), pad_start, pad_n, n_tiles


def _row_copy(src, src_row, dst, dst_row, sem):
    return pltpu.make_async_copy(src.at[_tile_rows(src_row)], dst.at[_tile_rows(dst_row)], sem)


def _scatter_kernel(pad_start_ref, pad_n_ref, nact_ref, pos0_ref, pos1_ref, h_ref, xs_ref,
                    zero_ref, stage_ref, sem, sems, *, tg):
    tm = h_ref.shape[0] // ROW_SUB
    half = tg // 2

    @pl.when(pl.program_id(0) == 0)
    def _():
        zero_ref[...] = jnp.zeros_like(zero_ref)

        def fill_tile(j, c):
            cps = [pltpu.make_async_copy(zero_ref, xs_ref.at[_tile_rows(j * tg + h * half, half)], sem)
                   for h in range(2)]
            for cp in cps:
                cp.start()
            for cp in cps:
                cp.wait()
            return c

        lax.fori_loop(nact_ref[0], xs_ref.shape[0] // (tg * ROW_SUB), fill_tile, 0)

    @pl.when(pl.program_id(0) == 0)
    def _():
        for e in range(N_EXPERTS):
            n = pad_n_ref[e]
            off = pad_start_ref[e]
            b = 1
            while b < tg:
                hit = (n & b) != 0

                @pl.when(hit)
                def _(off=off, b=b):
                    cp = pltpu.make_async_copy(zero_ref.at[_tile_rows(0, b)], xs_ref.at[_tile_rows(off, b)], sem)
                    cp.start()
                    cp.wait()

                off = off + jnp.where(hit, b, 0)
                b *= 2

    step = pl.program_id(0)
    slot = step % 2
    stage = stage_ref.at[slot]
    stage[...] = h_ref[...]

    def issue(r, c):
        _row_copy(stage, r, xs_ref, pos0_ref[r], sems.at[slot]).start(priority=0)
        _row_copy(stage, r, xs_ref, pos1_ref[r], sems.at[slot]).start(priority=1)
        return c

    lax.fori_loop(0, tm, issue, 0, unroll=8)

    def drain(which):
        def body(r, c):
            _row_copy(stage, 0, xs_ref, 0, sems.at[which]).wait()
            _row_copy(stage, 0, xs_ref, 0, sems.at[which]).wait()
            return c

        lax.fori_loop(0, tm, body, 0, unroll=8)

    @pl.when(step > 0)
    def _():
        drain(1 - slot)

    @pl.when(step == pl.num_programs(0) - 1)
    def _():
        drain(slot)


def _scatter_rows(h2, pos0, pos1, pad_start, pad_n, n_act, n_rows, tm, tg):
    N = h2.shape[0] // ROW_SUB
    tok = lambda i, ps, pn, na: (i,)
    return pl.pallas_call(
        functools.partial(_scatter_kernel, tg=tg),
        grid_spec=pltpu.PrefetchScalarGridSpec(
            num_scalar_prefetch=3,
            grid=(N // tm,),
            in_specs=[
                pl.BlockSpec((tm,), tok, memory_space=pltpu.SMEM),
                pl.BlockSpec((tm,), tok, memory_space=pltpu.SMEM),
                pl.BlockSpec((tm * ROW_SUB, LANES), lambda i, ps, pn, na: (i, 0)),
            ],
            out_specs=pl.BlockSpec(memory_space=pl.ANY),
            scratch_shapes=[pltpu.VMEM((tg // 2 * ROW_SUB, LANES), F32), pltpu.VMEM((2, tm * ROW_SUB, LANES), F32),
                            pltpu.SemaphoreType.DMA, pltpu.SemaphoreType.DMA((2,))],
        ),
        out_shape=jax.ShapeDtypeStruct((n_rows * ROW_SUB, LANES), F32),
        compiler_params=_cparams(("arbitrary",)),
        name="moe_scatter",
    )(pad_start, pad_n, n_act, pos0, pos1, h2)


def _experts_kernel(te_ref, nact_ref, x_ref, wg_ref, wu_ref, wd_ref, y_ref, wg_s, wu_s, wd_s):
    j = pl.program_id(0)
    active = j < nact_ref[0]
    new_expert = (j == 0) | (te_ref[j] != te_ref[jnp.maximum(j - 1, 0)])

    @pl.when(active & new_expert)
    def _():
        wg_s[...] = wg_ref[...].astype(BF16)
        wu_s[...] = wu_ref[...].astype(BF16)
        wd_s[...] = wd_ref[...].astype(BF16)

    @pl.when(active)
    def _():
        x = _from_tiles(x_ref, x_ref.shape[0] // ROW_SUB).astype(BF16)
        gt = jnp.dot(x, wg_s[...], preferred_element_type=F32)
        up = jnp.dot(x, wu_s[...], preferred_element_type=F32)
        hid = (gt * jax.nn.sigmoid(gt)) * up
        _to_tiles(y_ref, jnp.dot(hid.astype(BF16), wd_s[...], preferred_element_type=F32))

    @pl.when(jnp.logical_not(active))
    def _():
        y_ref[...] = jnp.zeros_like(y_ref)


def _experts(xs, tile_expert, n_act, w_gate, w_up, w_down, tg):
    R, D = xs.shape[0] // ROW_SUB, D_MODEL
    rows = lambda j, te, na: (j, 0)
    wmap = lambda j, te, na: (te[j], 0, 0)
    return pl.pallas_call(
        _experts_kernel,
        grid_spec=pltpu.PrefetchScalarGridSpec(
            num_scalar_prefetch=2,
            grid=(R // tg,),
            in_specs=[
                pl.BlockSpec((tg * ROW_SUB, LANES), rows),
                pl.BlockSpec((None, D, D_EXPERT), wmap),
                pl.BlockSpec((None, D, D_EXPERT), wmap),
                pl.BlockSpec((None, D_EXPERT, D), wmap),
            ],
            out_specs=pl.BlockSpec((tg * ROW_SUB, LANES), rows),
            scratch_shapes=[pltpu.VMEM((D, D_EXPERT), BF16), pltpu.VMEM((D, D_EXPERT), BF16),
                            pltpu.VMEM((D_EXPERT, D), BF16)],
        ),
        out_shape=jax.ShapeDtypeStruct((R * ROW_SUB, LANES), F32),
        compiler_params=_cparams(("arbitrary",)),
        name="moe_experts",
    )(tile_expert, n_act, xs, w_gate, w_up, w_down)


def _combine_kernel(pos0_ref, pos1_ref, pos0n_ref, pos1n_ref, route_ref, x2_ref, fg_ref, ys_ref, o_ref,
                    buf_ref, sems):
    tm = x2_ref.shape[0]
    step = pl.program_id(0)
    slot = step % 2

    def issue(p0_ref, p1_ref, which):
        def body(r, c):
            _row_copy(ys_ref, p0_ref[r], buf_ref.at[which, 0], r, sems.at[which]).start(priority=0)
            _row_copy(ys_ref, p1_ref[r], buf_ref.at[which, 1], r, sems.at[which]).start(priority=1)
            return c

        lax.fori_loop(0, tm, body, 0, unroll=8)

    @pl.when(step == 0)
    def _():
        issue(pos0_ref, pos1_ref, slot)

    @pl.when(step + 1 < pl.num_programs(0))
    def _():
        issue(pos0n_ref, pos1n_ref, 1 - slot)

    def drain(r, c):
        _row_copy(ys_ref, 0, buf_ref.at[slot, 0], 0, sems.at[slot]).wait()
        _row_copy(ys_ref, 0, buf_ref.at[slot, 0], 0, sems.at[slot]).wait()
        return c

    lax.fori_loop(0, tm, drain, 0, unroll=8)
    route = route_ref[...].T
    w1 = route[:, R_W:R_W + 1]
    w2 = route[:, R_W + 1:R_W + 2]
    y = x2_ref[...] + w1 * _from_tiles(buf_ref.at[slot, 0], tm) + w2 * _from_tiles(buf_ref.at[slot, 1], tm)
    ms = jnp.mean(y * y, axis=-1, keepdims=True)
    o_ref[...] = y * lax.rsqrt(ms + EPS) * fg_ref[...]


def _combine(ys, pos0, pos1, route, x2, final_g, tm):
    N, D = x2.shape
    n = N // tm
    row = lambda i: (i, 0)
    cur = lambda i: (i,)
    nxt = lambda i: (jnp.minimum(i + 1, n - 1),)
    smem = functools.partial(pl.BlockSpec, (tm,), memory_space=pltpu.SMEM)
    return pl.pallas_call(
        _combine_kernel,
        grid=(n,),
        in_specs=[
            smem(cur), smem(cur), smem(nxt), smem(nxt),
            pl.BlockSpec((SUBLANES, tm), lambda i: (0, i)),
            pl.BlockSpec((tm, D), row),
            pl.BlockSpec((1, D), lambda i: (0, 0)),
            pl.BlockSpec(memory_space=pl.ANY),
        ],
        out_specs=pl.BlockSpec((tm, D), row),
        out_shape=jax.ShapeDtypeStruct((N, D), F32),
        scratch_shapes=[pltpu.VMEM((2, 2, tm * ROW_SUB, LANES), F32), pltpu.SemaphoreType.DMA((2,))],
        compiler_params=_cparams(("arbitrary",)),
        name="moe_combine",
    )(pos0, pos1, pos0, pos1, route, x2, final_g.reshape(1, D), ys)


def _tiles(B, S):
    return dict(
        tm_in=min(512, S),
        tq=min(512, S),
        kc=min(256, S),
        tcp=min(32, S // 16),
        tm_out=min(512, S),
        tm_row=min(512, S),
        tg=min(512, S),
    )


def kernel(x, norm1_g, w_in, lambda_q1, lambda_k1, lambda_q2, lambda_k2, subln_g, conv_w, conv_b,
           lru_w_r, lru_b_r, lru_w_i, lru_b_i, lru_lambda, w_out, norm2_g, w_grp, w_exp, w_gate, w_up,
           w_down, final_g):
    B, S, D = x.shape
    t = _tiles(B, S)
    l = 0
    q, k, vt, xr, gate = _inproj(x, norm1_g[l], w_in[l], t["tm_in"])
    lam_vecs = jnp.zeros((SUBLANES, LANES), F32).at[0:4, :HEAD_DIM].set(
        jnp.stack([lambda_q1[l], lambda_k1[l], lambda_q2[l], lambda_k2[l]]))
    attn = _attention(q, k, vt, lam_vecs, subln_g[l], t["tq"], t["kc"])
    rnn = _rglru(xr, gate, conv_w[l], conv_b[l], lru_w_r[l], lru_b_r[l], lru_w_i[l], lru_b_i[l],
                 lru_lambda[l], t["tcp"])
    N = B * S
    x2, h2, route_t, counts = _outproj_router(
        attn.reshape(N, ATTN_WIDTH), rnn.reshape(N, LRU_WIDTH), x.reshape(N, D), w_out[l], norm2_g[l],
        w_grp[l], w_exp[l], t["tm_out"])
    tg = t["tg"]
    pos0, pos1, tile_expert, n_act, pad_start, pad_n, n_tiles = _dispatch_plan(route_t, counts, tg)
    xs = _scatter_rows(h2, pos0, pos1, pad_start, pad_n, n_act, n_tiles * tg, t["tm_row"], tg)
    ys = _experts(xs, tile_expert, n_act, w_gate[l], w_up[l], w_down[l], tg)
    out = _combine(ys, pos0, pos1, route_t, x2, final_g, t["tm_row"])
    return out.reshape(B, S, D)
```

```python
import functools
import math

import jax
import jax.numpy as jnp
from jax import lax
from jax.experimental import pallas as pl
from jax.experimental.pallas import tpu as pltpu

F32 = jnp.float32
BF16 = jnp.bfloat16

D_MODEL = 1024
ATTN_HEADS = 4
HEAD_DIM = 64
V_DIM = 2 * HEAD_DIM
ATTN_WIDTH = ATTN_HEADS * V_DIM
ROPE_THETA = 500000.0
ROPE_DIM = HEAD_DIM // 4
LRU_WIDTH = D_MODEL - ATTN_WIDTH
LRU_BLOCK_DIM = 64
CONV_WIDTH = 4
LRU_C = 8.0
N_GROUPS = 4
EXPERTS_PER_GROUP = 8
N_EXPERTS = N_GROUPS * EXPERTS_PER_GROUP
D_EXPERT = 256
EPS = 1e-6
SQRT_GUARD = 1e-37
LAMBDA_INIT = 0.8 - 0.6 * math.exp(0.0)

LANES = 128
SUBLANES = 8
MXU_DIM = 256
BF16_SUBLANES = 16
VT_ROWS = V_DIM + BF16_SUBLANES
ROW_SUB = D_MODEL // LANES
assert ROW_SUB == SUBLANES
VMEM_LIMIT = 56 * 1024 * 1024


def _cparams(semantics, flags=None):
    return pltpu.CompilerParams(dimension_semantics=semantics, vmem_limit_bytes=VMEM_LIMIT, flags=flags)


def _rope_tables(seq_len):
    pos = jnp.arange(seq_len, dtype=F32)
    inv_freq = ROPE_THETA ** (-jnp.arange(0, ROPE_DIM, 2, dtype=F32) / ROPE_DIM)
    ang = pos[:, None] * inv_freq[None, :]
    cos, sin = jnp.cos(ang), jnp.sin(ang)
    half = ROPE_DIM // 2
    j = jnp.arange(LANES) % HEAD_DIM
    f = j % half
    c = jnp.where(j[None, :] < ROPE_DIM, cos[:, f], 1.0)
    s_lo = jnp.where(j[None, :] < half, -sin[:, f], 0.0)
    s_hi = jnp.where((j[None, :] >= half) & (j[None, :] < ROPE_DIM), sin[:, f], 0.0)
    return c, s_lo, s_hi


def _inproj_kernel(x_ref, g_ref, w_ref, wvt_ref, c_ref, slo_ref, shi_ref,
                   q_ref, k_ref, vt_ref, xr_ref, gate_ref):
    x = x_ref[...]
    ms = jnp.mean(x * x, axis=-1, keepdims=True)
    h = (x * lax.rsqrt(ms + EPS) * g_ref[...]).astype(BF16)
    c, slo, shi = c_ref[...], slo_ref[...], shi_ref[...]

    def rope(p):
        return p * c + pltpu.roll(p, LANES - ROPE_DIM // 2, 1) * slo + pltpu.roll(p, ROPE_DIM // 2, 1) * shi

    q_scale = HEAD_DIM ** -0.5 * math.log2(math.e)
    pq = jnp.dot(h, w_ref[:, 0:ATTN_WIDTH], preferred_element_type=F32)
    pk = jnp.dot(h, w_ref[:, ATTN_WIDTH:2 * ATTN_WIDTH], preferred_element_type=F32)
    for hd in range(ATTN_HEADS):
        lo = hd * V_DIM
        q_ref[:, lo:lo + V_DIM] = (rope(pq[:, lo:lo + V_DIM]) * q_scale).astype(BF16)
        k_ref[:, lo:lo + V_DIM] = rope(pk[:, lo:lo + V_DIM]).astype(BF16)
    vt = lax.dot_general(wvt_ref[...], h, (((1,), (1,)), ((), ())), preferred_element_type=F32).astype(BF16)
    for hd in range(ATTN_HEADS):
        vt_ref[hd * VT_ROWS:hd * VT_ROWS + V_DIM, :] = vt[hd * V_DIM:(hd + 1) * V_DIM]
        vt_ref[hd * VT_ROWS + V_DIM:(hd + 1) * VT_ROWS, :] = jnp.ones((VT_ROWS - V_DIM, vt.shape[1]), BF16)
    off = 3 * ATTN_WIDTH
    xr_ref[...] = jnp.dot(h, w_ref[:, off:off + LRU_WIDTH], preferred_element_type=F32).astype(BF16)
    off += LRU_WIDTH
    gate_ref[...] = jnp.dot(h, w_ref[:, off:off + LRU_WIDTH], preferred_element_type=F32).astype(BF16)


def _inproj(x, g, w_in, tm):
    B, S, D = x.shape
    c, slo, shi = _rope_tables(S)
    row = lambda si, b: (b, si, 0)
    tab = lambda si, b: (si, 0)
    const = lambda si, b: (0, 0)
    out = jax.ShapeDtypeStruct((B, S, ATTN_WIDTH), BF16)
    w_in = w_in.astype(BF16)
    wvt = w_in[:, 2 * ATTN_WIDTH:3 * ATTN_WIDTH].T
    return pl.pallas_call(
        _inproj_kernel,
        grid=(S // tm, B),
        in_specs=[
            pl.BlockSpec((None, tm, D), row),
            pl.BlockSpec((1, D), const),
            pl.BlockSpec(w_in.shape, const),
            pl.BlockSpec(wvt.shape, const),
            pl.BlockSpec((tm, LANES), tab),
            pl.BlockSpec((tm, LANES), tab),
            pl.BlockSpec((tm, LANES), tab),
        ],
        out_specs=[pl.BlockSpec((None, tm, ATTN_WIDTH), row)] * 2
        + [pl.BlockSpec((None, ATTN_HEADS * VT_ROWS, tm), lambda si, b: (b, 0, si))]
        + [pl.BlockSpec((None, tm, LRU_WIDTH), row)] * 2,
        out_shape=[out, out, jax.ShapeDtypeStruct((B, ATTN_HEADS * VT_ROWS, S), BF16), out, out],
        compiler_params=_cparams(("arbitrary", "arbitrary")),
        name="inproj",
    )(x, g.reshape(1, D), w_in, wvt, c, slo, shi)


def _attn_kernel(lam_ref, q_ref, k_ref, vt_ref, sg_ref, o_ref, *, kc, lookahead):
    lv = lam_ref[...]
    lam = (jnp.exp(jnp.sum(lv[0:1] * lv[1:2], axis=-1, keepdims=True))
           - jnp.exp(jnp.sum(lv[2:3] * lv[3:4], axis=-1, keepdims=True)) + LAMBDA_INIT)
    q = q_ref[...]
    tq = q.shape[0]
    S = k_ref.shape[0]
    lane = lax.broadcasted_iota(jnp.int32, q.shape, 1)
    zero = jnp.zeros_like(q)
    qs = (jnp.where(lane < HEAD_DIM, q, zero), jnp.where(lane >= HEAD_DIM, q, zero))
    m = [jnp.full((1, tq), -1e30, F32) for _ in qs]
    acc = [jnp.zeros((VT_ROWS, tq), F32) for _ in qs]
    n_chunks = S // kc

    def scores(c):
        k_c = k_ref[c * kc:(c + 1) * kc, :]
        return [lax.dot_general(k_c, qm, (((1,), (1,)), ((), ())), preferred_element_type=F32) for qm in qs]

    ahead = [scores(c) for c in range(min(lookahead, n_chunks))]
    for c in range(n_chunks):
        if c + lookahead < n_chunks:
            ahead.append(scores(c + lookahead))
        vt_c = vt_ref[:, c * kc:(c + 1) * kc]
        for j, s in enumerate(ahead.pop(0)):
            m_new = jnp.maximum(m[j], jnp.max(s, axis=0, keepdims=True))
            alpha = jnp.exp2(m[j] - m_new)
            p = jnp.exp2((s - m_new).astype(BF16))
            acc[j] = alpha * acc[j] + jnp.dot(vt_c, p, preferred_element_type=F32)
            m[j] = m_new
    o = (acc[0][:V_DIM] / acc[0][V_DIM:V_DIM + 1]
         - lam * (acc[1][:V_DIM] / acc[1][V_DIM:V_DIM + 1]))
    ms = jnp.mean(o * o, axis=0, keepdims=True)
    o = (o * lax.rsqrt(ms + EPS)).T
    o_ref[...] = (o * sg_ref[...] * (1.0 - LAMBDA_INIT)).astype(o_ref.dtype)


def _attention(q, k, vt, lam_vecs, subln_g, tq, kc):
    B, S, _ = q.shape
    qmap = lambda b, h, i: (b, i, h)
    const = lambda b, h, i: (0, 0)
    return pl.pallas_call(
        functools.partial(_attn_kernel, kc=kc, lookahead=2),
        grid=(B, ATTN_HEADS, S // tq),
        in_specs=[
            pl.BlockSpec((SUBLANES, LANES), const),
            pl.BlockSpec((None, tq, V_DIM), qmap),
            pl.BlockSpec((None, S, V_DIM), lambda b, h, i: (b, 0, h)),
            pl.BlockSpec((None, VT_ROWS, S), lambda b, h, i: (b, h, 0)),
            pl.BlockSpec((1, V_DIM), const),
        ],
        out_specs=pl.BlockSpec((None, tq, V_DIM), qmap),
        out_shape=jax.ShapeDtypeStruct((B, S, ATTN_WIDTH), BF16),
        compiler_params=_cparams(("arbitrary", "arbitrary", "arbitrary")),
        name="diffattn",
    )(lam_vecs, q, k, vt, subln_g.reshape(1, V_DIM))


def _shift_rows(x, d, fill, up):
    n, c = x.shape
    if d % SUBLANES == 0:
        pad = jnp.full((d, c), fill, x.dtype)
        return jnp.concatenate([x[d:], pad]) if up else jnp.concatenate([pad, x[:n - d]])
    row = lax.broadcasted_iota(jnp.int32, (SUBLANES, c), 0)
    if up:
        r = pltpu.roll(x, n - d, 0)
        last = jnp.where(row < SUBLANES - d, r[n - SUBLANES:], fill)
        return last if n == SUBLANES else jnp.concatenate([r[:n - SUBLANES], last])
    r = pltpu.roll(x, d, 0)
    first = jnp.where(row >= d, r[:SUBLANES], fill)
    return first if n == SUBLANES else jnp.concatenate([first, r[SUBLANES:]])


def _chunk_scan(a, u, reverse):
    n = a.shape[0]
    d = 1
    while d < n:
        a_s = _shift_rows(a, d, 1.0, reverse)
        u_s = _shift_rows(u, d, 0.0, reverse)
        u = a * u_s + u
        a = a * a_s
        d *= 2
    return a, u


def _rglru_kernel(xr_ref, gate_ref, cw_ref, cb_ref, wg_ref, bg_ref, coef_ref, o_ref,
                  xp_ref, gp_ref, hf_ref, pf_ref, hb_ref, pb_ref, *, tcp):
    S, C = xr_ref.shape
    SEG = SUBLANES
    L = S // SEG
    blk = BF16_SUBLANES
    n = tcp * SEG
    halo_lo = 2 * SEG

    def permute_in(b, c):
        for j in range(SEG):
            src = pl.multiple_of(j * L + b * blk, blk)
            dst = (b * blk) * SEG + j
            xp_ref[pl.ds(halo_lo + dst, blk, stride=SEG), :] = xr_ref[pl.ds(src, blk), :].astype(F32)
            gp_ref[pl.ds(dst, blk, stride=SEG), :] = gate_ref[pl.ds(src, blk), :].astype(F32)
        return c

    lax.fori_loop(0, L // blk, permute_in, 0)
    row = lax.broadcasted_iota(jnp.int32, (SEG, C), 0)
    for k in (1, 2):
        tail = xp_ref[pl.ds(halo_lo + (L - k) * SEG, SEG), :]
        xp_ref[pl.ds(halo_lo - k * SEG, SEG), :] = jnp.where(row >= 1, pltpu.roll(tail, 1, 0), 0.0)
    head = xp_ref[pl.ds(halo_lo, SEG), :]
    xp_ref[pl.ds(halo_lo + L * SEG, SEG), :] = jnp.where(row < SEG - 1, pltpu.roll(head, SEG - 1, 0), 0.0)

    cw = cw_ref[...]
    cb = cb_ref[...]
    bg = bg_ref[...]
    coef = coef_ref[...]

    def decay_and_input(r0, direction):
        w = xp_ref[pl.ds(r0, n + 3 * SEG), :]
        xc = (w[0:n] * cw[0:1] + w[SEG:SEG + n] * cw[1:2] + w[2 * SEG:2 * SEG + n] * cw[2:3]
              + w[3 * SEG:3 * SEG + n] * cw[3:4]) + cb
        lin = jnp.dot(xc.astype(BF16), wg_ref[:, 2 * C * direction:2 * C * (direction + 1)],
                      preferred_element_type=F32)
        r = jax.nn.sigmoid(lin[:, :C] + bg[2 * direction:2 * direction + 1])
        i = jax.nn.sigmoid(lin[:, C:] + bg[2 * direction + 1:2 * direction + 2])
        log_a = coef[direction:direction + 1] * r
        th = jnp.tanh(log_a)
        one_m_a2 = (-2.0 * th) * pl.reciprocal(1.0 - th, approx=True)
        root = one_m_a2 * lax.rsqrt(jnp.maximum(one_m_a2, SQRT_GUARD))
        return jnp.exp(log_a), root * (i * xc)

    n_chunks = L // tcp

    def scan_chunk(c_idx, direction, h, p, h_ref, p_ref):
        r0 = pl.multiple_of(c_idx * n, n)
        a, u = decay_and_input(r0, direction)
        hs, ps = [None] * tcp, [None] * tcp
        for s in (range(tcp) if direction == 0 else reversed(range(tcp))):
            a_s = a[s * SEG:(s + 1) * SEG]
            h = a_s * h + u[s * SEG:(s + 1) * SEG]
            p = a_s * p
            hs[s], ps[s] = h, p
        h_ref[pl.ds(r0, n), :] = jnp.concatenate(hs)
        p_ref[pl.ds(r0, n), :] = jnp.concatenate(ps)
        return h, p

    def scan_both(ci, carry):
        hf, pf, hb, pb = carry
        hf, pf = scan_chunk(ci, 0, hf, pf, hf_ref, pf_ref)
        hb, pb = scan_chunk(n_chunks - 1 - ci, 1, hb, pb, hb_ref, pb_ref)
        return hf, pf, hb, pb

    zero, one = jnp.zeros((SEG, C), F32), jnp.ones((SEG, C), F32)
    hf_end, pf_end, hb_end, pb_end = lax.fori_loop(0, n_chunks, scan_both, (zero, one, zero, one), unroll=2)
    c_f = _shift_rows(_chunk_scan(pf_end, hf_end, False)[1], 1, 0.0, False)
    c_b = _shift_rows(_chunk_scan(pb_end, hb_end, True)[1], 1, 0.0, True)

    def finish(ci, c):
        rows = pl.ds(pl.multiple_of(ci * n, n), n)
        cf = jnp.concatenate([c_f] * tcp)
        cbk = jnp.concatenate([c_b] * tcp)
        h = hf_ref[rows, :] + pf_ref[rows, :] * cf + hb_ref[rows, :] + pb_ref[rows, :] * cbk
        gp_ref[rows, :] = h * jax.nn.gelu(gp_ref[rows, :])
        return c

    lax.fori_loop(0, L // tcp, finish, 0)

    def permute_out(b, c):
        for j in range(SEG):
            dst = pl.multiple_of(j * L + b * blk, blk)
            o_ref[pl.ds(dst, blk), :] = gp_ref[pl.ds((b * blk) * SEG + j, blk, stride=SEG), :].astype(o_ref.dtype)
        return c

    lax.fori_loop(0, L // blk, permute_out, 0)


def _block_diag(w):
    nb, d, _ = w.shape
    eye = jnp.eye(nb, dtype=w.dtype)
    return (eye[:, None, :, None] * w[:, :, None, :]).reshape(nb * d, nb * d)


def _rglru(xr, gate, conv_w, conv_b, w_r, b_r, w_i, b_i, lru_lambda, tcp):
    B, S, W = xr.shape
    C = LANES
    n_c = W // C
    mats = [_block_diag(w_r[0]), _block_diag(w_i[0]), _block_diag(w_r[1]), _block_diag(w_i[1])]
    wg = jnp.stack([jnp.concatenate([m[c * C:(c + 1) * C, c * C:(c + 1) * C] for m in mats], axis=1)
                    for c in range(n_c)]).astype(BF16)
    bg = jnp.stack([b_r[0], b_i[0], b_r[1], b_i[1]])
    coef = -LRU_C * jax.nn.softplus(-lru_lambda.astype(F32))
    blk = lambda b, c: (b, 0, c)
    par = lambda b, c: (0, c)
    return pl.pallas_call(
        functools.partial(_rglru_kernel, tcp=tcp),
        grid=(B, n_c),
        in_specs=[
            pl.BlockSpec((None, S, C), blk),
            pl.BlockSpec((None, S, C), blk),
            pl.BlockSpec((CONV_WIDTH, C), par),
            pl.BlockSpec((1, C), par),
            pl.BlockSpec((None, C, 4 * C), lambda b, c: (c, 0, 0)),
            pl.BlockSpec((4, C), par),
            pl.BlockSpec((2, C), par),
        ],
        out_specs=pl.BlockSpec((None, S, C), blk),
        out_shape=jax.ShapeDtypeStruct((B, S, W), BF16),
        scratch_shapes=[pltpu.VMEM((S + (CONV_WIDTH - 1) * SUBLANES, C), F32)] + [pltpu.VMEM((S, C), F32)] * 5,
        compiler_params=_cparams(("arbitrary", "arbitrary")),
        name="rglru",
    )(xr, gate, conv_w, conv_b.reshape(1, W), wg, bg, coef)


def _to_tiles(ref, val):
    rows = val.shape[0]
    for j in range(ROW_SUB):
        ref[pl.ds(j, rows, stride=ROW_SUB), :] = val[:, j * LANES:(j + 1) * LANES]


def _from_tiles(ref, rows):
    return jnp.concatenate([ref[pl.ds(j, rows, stride=ROW_SUB), :] for j in range(ROW_SUB)], axis=1)


def _tile_rows(r, n=1):
    start = r * ROW_SUB if isinstance(r, int) else pl.multiple_of(r * ROW_SUB, ROW_SUB)
    return pl.ds(start, n * ROW_SUB)


R_ID, R_W, R_RANK = 0, 2, 4


def _outproj_kernel(attn_ref, rnn_ref, x_ref, wa_ref, wr_ref, g_ref, wrt_ref, triu_ref,
                    x2_ref, h2_ref, route_t_ref, cnt_ref, run_ref):
    @pl.when(pl.program_id(0) == 0)
    def _():
        run_ref[...] = jnp.zeros_like(run_ref)

    mix = (jnp.dot(attn_ref[...], wa_ref[...], preferred_element_type=F32)
           + jnp.dot(rnn_ref[...], wr_ref[...], preferred_element_type=F32))
    x2 = x_ref[...] + mix
    x2_ref[...] = x2
    ms = jnp.mean(x2 * x2, axis=-1, keepdims=True)
    h2 = x2 * lax.rsqrt(ms + EPS) * g_ref[...]
    _to_tiles(h2_ref, h2)
    lt = lax.dot_general(wrt_ref[...], h2.astype(BF16), (((1,), (1,)), ((), ())),
                         preferred_element_type=F32)
    tm = lt.shape[1]
    le = lt[0:N_EXPERTS]
    lg = lt[N_EXPERTS:N_EXPERTS + SUBLANES]
    neg = jnp.float32(-jnp.inf)
    big = jnp.int32(LANES)
    grow = lax.broadcasted_iota(jnp.int32, lg.shape, 0)
    is_g = grow < N_GROUPS
    gl = jnp.where(is_g, lg, neg)
    gmax = jnp.max(gl, axis=0, keepdims=True)
    gsum = jnp.sum(jnp.where(is_g, jnp.exp(gl - gmax), 0.0), axis=0, keepdims=True)
    g_top_p = 1.0 / gsum
    g_idx = jnp.min(jnp.where(is_g & (gl == gmax), grow, big), axis=0, keepdims=True)
    erow = lax.broadcasted_iota(jnp.int32, le.shape, 0)
    e_lo = g_idx * EXPERTS_PER_GROUP
    in_grp = (erow >= e_lo) & (erow < e_lo + EXPERTS_PER_GROUP)
    el = jnp.where(in_grp, le, neg)
    e1 = jnp.max(el, axis=0, keepdims=True)
    i1 = jnp.min(jnp.where(in_grp & (el == e1), erow, big), axis=0, keepdims=True)
    el2 = jnp.where(erow == i1, neg, el)
    e2 = jnp.max(el2, axis=0, keepdims=True)
    i2 = jnp.min(jnp.where(in_grp & (erow != i1) & (el2 == e2), erow, big), axis=0, keepdims=True)
    t = jnp.exp(e2 - e1)
    w1 = g_top_p / (1.0 + t)
    w2 = g_top_p * t / (1.0 + t)
    oh1 = erow == i1
    oh2 = erow == i2
    oh = jnp.where(oh1 | oh2, 1.0, 0.0)
    run = run_ref[:, 0:1]
    before = jnp.dot(oh.astype(BF16), triu_ref[...], preferred_element_type=F32) + run
    rank1 = jnp.sum(jnp.where(oh1, before, 0.0), axis=0, keepdims=True)
    rank2 = jnp.sum(jnp.where(oh2, before, 0.0), axis=0, keepdims=True)
    run_ref[...] = run_ref[...] + jnp.sum(oh, axis=1, keepdims=True)
    cnt_ref[...] = run_ref[...]
    fields = [None] * SUBLANES
    fields[R_ID], fields[R_ID + 1] = i1.astype(F32), i2.astype(F32)
    fields[R_W], fields[R_W + 1] = w1, w2
    fields[R_RANK], fields[R_RANK + 1] = rank1, rank2
    zero = jnp.zeros((1, tm), F32)
    route_t_ref[...] = jnp.concatenate([zero if f is None else f for f in fields], axis=0)


def _outproj_router(attn, rnn, x, w_out, g2, w_grp, w_exp, tm):
    N, D = x.shape
    w_out = w_out.astype(BF16)
    wrt = jnp.zeros((N_EXPERTS + SUBLANES, D), F32).at[:N_EXPERTS].set(w_exp.T)
    wrt = wrt.at[N_EXPERTS:N_EXPERTS + N_GROUPS].set(w_grp.T).astype(BF16)
    triu = jnp.triu(jnp.ones((tm, tm), BF16), 1)
    row = lambda i: (i, 0)
    const = lambda i: (0, 0)
    return pl.pallas_call(
        _outproj_kernel,
        grid=(N // tm,),
        in_specs=[
            pl.BlockSpec((tm, ATTN_WIDTH), row),
            pl.BlockSpec((tm, LRU_WIDTH), row),
            pl.BlockSpec((tm, D), row),
            pl.BlockSpec((ATTN_WIDTH, D), const),
            pl.BlockSpec((LRU_WIDTH, D), const),
            pl.BlockSpec((1, D), const),
            pl.BlockSpec(wrt.shape, const),
            pl.BlockSpec((tm, tm), const),
        ],
        out_specs=[pl.BlockSpec((tm, D), row), pl.BlockSpec((tm * ROW_SUB, LANES), row),
                   pl.BlockSpec((SUBLANES, tm), lambda i: (0, i)), pl.BlockSpec((N_EXPERTS, LANES), const)],
        out_shape=[jax.ShapeDtypeStruct((N, D), F32), jax.ShapeDtypeStruct((N * ROW_SUB, LANES), F32),
                   jax.ShapeDtypeStruct((SUBLANES, N), F32), jax.ShapeDtypeStruct((N_EXPERTS, LANES), F32)],
        scratch_shapes=[pltpu.VMEM((N_EXPERTS, LANES), F32)],
        compiler_params=_cparams(("arbitrary",)),
        name="outproj_router",
    )(attn, rnn, x, w_out[:ATTN_WIDTH], w_out[ATTN_WIDTH:], g2.reshape(1, D), wrt, triu)


def _dispatch_plan(route_t, counts, tg):
    N = route_t.shape[1]
    ids = route_t[R_ID:R_ID + 2].astype(jnp.int32)
    rank = route_t[R_RANK:R_RANK + 2].astype(jnp.int32)
    cnt = counts[:, 0].astype(jnp.int32)
    tiles_e = (cnt + tg - 1) // tg
    tile_end = jnp.cumsum(tiles_e)
    row_off = (tile_end - tiles_e) * tg
    pos = rank
    for e in range(N_EXPERTS):
        pos = pos + jnp.where(ids == e, row_off[e], 0)
    n_tiles = (2 * N) // tg + N_EXPERTS
    n_act = tile_end[-1]
    tile_id = jnp.minimum(jnp.arange(n_tiles, dtype=jnp.int32), n_act - 1)
    tile_expert = jnp.sum((tile_id[:, None] >= tile_end[None, :]).astype(jnp.int32), axis=1)
    pad_start = row_off + cnt
    pad_n = tiles_e * tg - cnt
    return pos[0], pos[1], tile_expert, n_act.reshape(1).astype(jnp.int32), pad_start, pad_n, n_tiles


def _row_copy(src, src_row, dst, dst_row, sem):
    return pltpu.make_async_copy(src.at[_tile_rows(src_row)], dst.at[_tile_rows(dst_row)], sem)


def _scatter_kernel(pad_start_ref, pad_n_ref, nact_ref, pos0_ref, pos1_ref, h_ref, xs_ref,
                    zero_ref, stage_ref, sem, sems, *, tg):
    tm = h_ref.shape[0] // ROW_SUB
    half = tg // 2

    @pl.when(pl.program_id(0) == 0)
    def _():
        zero_ref[...] = jnp.zeros_like(zero_ref)

        def fill_tile(j, c):
            cps = [pltpu.make_async_copy(zero_ref, xs_ref.at[_tile_rows(j * tg + h * half, half)], sem)
                   for h in range(2)]
            for cp in cps:
                cp.start()
            for cp in cps:
                cp.wait()
            return c

        lax.fori_loop(nact_ref[0], xs_ref.shape[0] // (tg * ROW_SUB), fill_tile, 0)

    @pl.when(pl.program_id(0) == 0)
    def _():
        for e in range(N_EXPERTS):
            n = pad_n_ref[e]
            off = pad_start_ref[e]
            b = 1
            while b < tg:
                hit = (n & b) != 0

                @pl.when(hit)
                def _(off=off, b=b):
                    cp = pltpu.make_async_copy(zero_ref.at[_tile_rows(0, b)], xs_ref.at[_tile_rows(off, b)], sem)
                    cp.start()
                    cp.wait()

                off = off + jnp.where(hit, b, 0)
                b *= 2

    step = pl.program_id(0)
    slot = step % 2
    stage = stage_ref.at[slot]
    stage[...] = h_ref[...]

    def issue(r, c):
        _row_copy(stage, r, xs_ref, pos0_ref[r], sems.at[slot]).start(priority=0)
        _row_copy(stage, r, xs_ref, pos1_ref[r], sems.at[slot]).start(priority=1)
        return c

    lax.fori_loop(0, tm, issue, 0, unroll=8)

    def drain(which):
        def body(r, c):
            _row_copy(stage, 0, xs_ref, 0, sems.at[which]).wait()
            _row_copy(stage, 0, xs_ref, 0, sems.at[which]).wait()
            return c

        lax.fori_loop(0, tm, body, 0, unroll=8)

    @pl.when(step > 0)
    def _():
        drain(1 - slot)

    @pl.when(step == pl.num_programs(0) - 1)
    def _():
        drain(slot)


def _scatter_rows(h2, pos0, pos1, pad_start, pad_n, n_act, n_rows, tm, tg):
    N = h2.shape[0] // ROW_SUB
    tok = lambda i, ps, pn, na: (i,)
    return pl.pallas_call(
        functools.partial(_scatter_kernel, tg=tg),
        grid_spec=pltpu.PrefetchScalarGridSpec(
            num_scalar_prefetch=3,
            grid=(N // tm,),
            in_specs=[
                pl.BlockSpec((tm,), tok, memory_space=pltpu.SMEM),
                pl.BlockSpec((tm,), tok, memory_space=pltpu.SMEM),
                pl.BlockSpec((tm * ROW_SUB, LANES), lambda i, ps, pn, na: (i, 0)),
            ],
            out_specs=pl.BlockSpec(memory_space=pl.ANY),
            scratch_shapes=[pltpu.VMEM((tg // 2 * ROW_SUB, LANES), F32), pltpu.VMEM((2, tm * ROW_SUB, LANES), F32),
                            pltpu.SemaphoreType.DMA, pltpu.SemaphoreType.DMA((2,))],
        ),
        out_shape=jax.ShapeDtypeStruct((n_rows * ROW_SUB, LANES), F32),
        compiler_params=_cparams(("arbitrary",)),
        name="moe_scatter",
    )(pad_start, pad_n, n_act, pos0, pos1, h2)


def _experts_kernel(te_ref, nact_ref, x_ref, wg_ref, wu_ref, wd_ref, y_ref, wg_s, wu_s, wd_s):
    j = pl.program_id(0)
    active = j < nact_ref[0]
    new_expert = (j == 0) | (te_ref[j] != te_ref[jnp.maximum(j - 1, 0)])

    @pl.when(active & new_expert)
    def _():
        wg_s[...] = wg_ref[...].astype(BF16)
        wu_s[...] = wu_ref[...].astype(BF16)
        wd_s[...] = wd_ref[...].astype(BF16)

    @pl.when(active)
    def _():
        x = _from_tiles(x_ref, x_ref.shape[0] // ROW_SUB).astype(BF16)
        gt = jnp.dot(x, wg_s[...], preferred_element_type=F32)
        up = jnp.dot(x, wu_s[...], preferred_element_type=F32)
        hid = (gt * jax.nn.sigmoid(gt)) * up
        _to_tiles(y_ref, jnp.dot(hid.astype(BF16), wd_s[...], preferred_element_type=F32))

    @pl.when(jnp.logical_not(active))
    def _():
        y_ref[...] = jnp.zeros_like(y_ref)


def _experts(xs, tile_expert, n_act, w_gate, w_up, w_down, tg):
    R, D = xs.shape[0] // ROW_SUB, D_MODEL
    rows = lambda j, te, na: (j, 0)
    wmap = lambda j, te, na: (te[j], 0, 0)
    return pl.pallas_call(
        _experts_kernel,
        grid_spec=pltpu.PrefetchScalarGridSpec(
            num_scalar_prefetch=2,
            grid=(R // tg,),
            in_specs=[
                pl.BlockSpec((tg * ROW_SUB, LANES), rows),
                pl.BlockSpec((None, D, D_EXPERT), wmap),
                pl.BlockSpec((None, D, D_EXPERT), wmap),
                pl.BlockSpec((None, D_EXPERT, D), wmap),
            ],
            out_specs=pl.BlockSpec((tg * ROW_SUB, LANES), rows),
            scratch_shapes=[pltpu.VMEM((D, D_EXPERT), BF16), pltpu.VMEM((D, D_EXPERT), BF16),
                            pltpu.VMEM((D_EXPERT, D), BF16)],
        ),
        out_shape=jax.ShapeDtypeStruct((R * ROW_SUB, LANES), F32),
        compiler_params=_cparams(("arbitrary",)),
        name="moe_experts",
    )(tile_expert, n_act, xs, w_gate, w_up, w_down)


def _combine_kernel(pos0_ref, pos1_ref, pos0n_ref, pos1n_ref, route_ref, x2_ref, fg_ref, ys_ref, o_ref,
                    buf_ref, sems):
    tm = x2_ref.shape[0]
    step = pl.program_id(0)
    slot = step % 2

    def issue(p0_ref, p1_ref, which):
        def body(r, c):
            _row_copy(ys_ref, p0_ref[r], buf_ref.at[which, 0], r, sems.at[which]).start(priority=0)
            _row_copy(ys_ref, p1_ref[r], buf_ref.at[which, 1], r, sems.at[which]).start(priority=1)
            return c

        lax.fori_loop(0, tm, body, 0, unroll=8)

    @pl.when(step == 0)
    def _():
        issue(pos0_ref, pos1_ref, slot)

    @pl.when(step + 1 < pl.num_programs(0))
    def _():
        issue(pos0n_ref, pos1n_ref, 1 - slot)

    def drain(r, c):
        _row_copy(ys_ref, 0, buf_ref.at[slot, 0], 0, sems.at[slot]).wait()
        _row_copy(ys_ref, 0, buf_ref.at[slot, 0], 0, sems.at[slot]).wait()
        return c

    lax.fori_loop(0, tm, drain, 0, unroll=8)
    route = route_ref[...].T
    w1 = route[:, R_W:R_W + 1]
    w2 = route[:, R_W + 1:R_W + 2]
    y = x2_ref[...] + w1 * _from_tiles(buf_ref.at[slot, 0], tm) + w2 * _from_tiles(buf_ref.at[slot, 1], tm)
    ms = jnp.mean(y * y, axis=-1, keepdims=True)
    o_ref[...] = y * lax.rsqrt(ms + EPS) * fg_ref[...]


def _combine(ys, pos0, pos1, route, x2, final_g, tm):
    N, D = x2.shape
    n = N // tm
    row = lambda i: (i, 0)
    cur = lambda i: (i,)
    nxt = lambda i: (jnp.minimum(i + 1, n - 1),)
    smem = functools.partial(pl.BlockSpec, (tm,), memory_space=pltpu.SMEM)
    return pl.pallas_call(
        _combine_kernel,
        grid=(n,),
        in_specs=[
            smem(cur), smem(cur), smem(nxt), smem(nxt),
            pl.BlockSpec((SUBLANES, tm), lambda i: (0, i)),
            pl.BlockSpec((tm, D), row),
            pl.BlockSpec((1, D), lambda i: (0, 0)),
            pl.BlockSpec(memory_space=pl.ANY),
        ],
        out_specs=pl.BlockSpec((tm, D), row),
        out_shape=jax.ShapeDtypeStruct((N, D), F32),
        scratch_shapes=[pltpu.VMEM((2, 2, tm * ROW_SUB, LANES), F32), pltpu.SemaphoreType.DMA((2,))],
        compiler_params=_cparams(("arbitrary",)),
        name="moe_combine",
    )(pos0, pos1, pos0, pos1, route, x2, final_g.reshape(1, D), ys)


def _tiles(B, S):
    return dict(
        tm_in=min(512, S),
        tq=min(1024, S),
        kc=min(256, S),
        tcp=min(32, S // 16),
        tm_out=min(512, S),
        tm_row=min(512, S),
        tg=min(512, S),
    )


def kernel(x, norm1_g, w_in, lambda_q1, lambda_k1, lambda_q2, lambda_k2, subln_g, conv_w, conv_b,
           lru_w_r, lru_b_r, lru_w_i, lru_b_i, lru_lambda, w_out, norm2_g, w_grp, w_exp, w_gate, w_up,
           w_down, final_g):
    B, S, D = x.shape
    t = _tiles(B, S)
    l = 0
    q, k, vt, xr, gate = _inproj(x, norm1_g[l], w_in[l], t["tm_in"])
    lam_vecs = jnp.zeros((SUBLANES, LANES), F32).at[0:4, :HEAD_DIM].set(
        jnp.stack([lambda_q1[l], lambda_k1[l], lambda_q2[l], lambda_k2[l]]))
    attn = _attention(q, k, vt, lam_vecs, subln_g[l], t["tq"], t["kc"])
    rnn = _rglru(xr, gate, conv_w[l], conv_b[l], lru_w_r[l], lru_b_r[l], lru_w_i[l], lru_b_i[l],
                 lru_lambda[l], t["tcp"])
    N = B * S
    x2, h2, route_t, counts = _outproj_router(
        attn.reshape(N, ATTN_WIDTH), rnn.reshape(N, LRU_WIDTH), x.reshape(N, D), w_out[l], norm2_g[l],
        w_grp[l], w_exp[l], t["tm_out"])
    tg = t["tg"]
    pos0, pos1, tile_expert, n_act, pad_start, pad_n, n_tiles = _dispatch_plan(route_t, counts, tg)
    xs = _scatter_rows(h2, pos0, pos1, pad_start, pad_n, n_act, n_tiles * tg, t["tm_row"], tg)
    ys = _experts(xs, tile_expert, n_act, w_gate[l], w_up[l], w_down[l], tg)
    out = _combine(ys, pos0, pos1, route_t, x2, final_g, t["tm_row"])
    return out.reshape(B, S, D)
```

```python
import functools
import math

import jax
import jax.numpy as jnp
from jax import lax
from jax.experimental import pallas as pl
from jax.experimental.pallas import tpu as pltpu

F32 = jnp.float32
BF16 = jnp.bfloat16

D_MODEL = 1024
ATTN_HEADS = 4
HEAD_DIM = 64
V_DIM = 2 * HEAD_DIM
ATTN_WIDTH = ATTN_HEADS * V_DIM
ROPE_THETA = 500000.0
ROPE_DIM = HEAD_DIM // 4
LRU_WIDTH = D_MODEL - ATTN_WIDTH
LRU_BLOCK_DIM = 64
CONV_WIDTH = 4
LRU_C = 8.0
N_GROUPS = 4
EXPERTS_PER_GROUP = 8
N_EXPERTS = N_GROUPS * EXPERTS_PER_GROUP
D_EXPERT = 256
EPS = 1e-6
SQRT_GUARD = 1e-37
LAMBDA_INIT = 0.8 - 0.6 * math.exp(0.0)

LANES = 128
SUBLANES = 8
MXU_DIM = 256
BF16_SUBLANES = 16
VT_ROWS = V_DIM + BF16_SUBLANES
ROW_SUB = D_MODEL // LANES
assert ROW_SUB == SUBLANES
VMEM_LIMIT = 56 * 1024 * 1024


def _cparams(semantics, flags=None):
    return pltpu.CompilerParams(dimension_semantics=semantics, vmem_limit_bytes=VMEM_LIMIT, flags=flags)


def _rope_tables(seq_len):
    pos = jnp.arange(seq_len, dtype=F32)
    inv_freq = ROPE_THETA ** (-jnp.arange(0, ROPE_DIM, 2, dtype=F32) / ROPE_DIM)
    ang = pos[:, None] * inv_freq[None, :]
    cos, sin = jnp.cos(ang), jnp.sin(ang)
    half = ROPE_DIM // 2
    j = jnp.arange(LANES) % HEAD_DIM
    f = j % half
    c = jnp.where(j[None, :] < ROPE_DIM, cos[:, f], 1.0)
    s_lo = jnp.where(j[None, :] < half, -sin[:, f], 0.0)
    s_hi = jnp.where((j[None, :] >= half) & (j[None, :] < ROPE_DIM), sin[:, f], 0.0)
    return c, s_lo, s_hi


def _inproj_kernel(x_ref, g_ref, w_ref, wvt_ref, c_ref, slo_ref, shi_ref,
                   q_ref, k_ref, vt_ref, xr_ref, gate_ref):
    x = x_ref[...]
    ms = jnp.mean(x * x, axis=-1, keepdims=True)
    h = (x * lax.rsqrt(ms + EPS) * g_ref[...]).astype(BF16)
    c, slo, shi = c_ref[...], slo_ref[...], shi_ref[...]

    def rope(p):
        return p * c + pltpu.roll(p, LANES - ROPE_DIM // 2, 1) * slo + pltpu.roll(p, ROPE_DIM // 2, 1) * shi

    q_scale = HEAD_DIM ** -0.5 * math.log2(math.e)
    pq = jnp.dot(h, w_ref[:, 0:ATTN_WIDTH], preferred_element_type=F32)
    pk = jnp.dot(h, w_ref[:, ATTN_WIDTH:2 * ATTN_WIDTH], preferred_element_type=F32)
    for hd in range(ATTN_HEADS):
        lo = hd * V_DIM
        q_ref[:, lo:lo + V_DIM] = (rope(pq[:, lo:lo + V_DIM]) * q_scale).astype(BF16)
        k_ref[:, lo:lo + V_DIM] = rope(pk[:, lo:lo + V_DIM]).astype(BF16)
    vt = lax.dot_general(wvt_ref[...], h, (((1,), (1,)), ((), ())), preferred_element_type=F32).astype(BF16)
    for hd in range(ATTN_HEADS):
        vt_ref[hd * VT_ROWS:hd * VT_ROWS + V_DIM, :] = vt[hd * V_DIM:(hd + 1) * V_DIM]
        vt_ref[hd * VT_ROWS + V_DIM:(hd + 1) * VT_ROWS, :] = jnp.ones((VT_ROWS - V_DIM, vt.shape[1]), BF16)
    off = 3 * ATTN_WIDTH
    xr_ref[...] = jnp.dot(h, w_ref[:, off:off + LRU_WIDTH], preferred_element_type=F32).astype(BF16)
    off += LRU_WIDTH
    gate_ref[...] = jnp.dot(h, w_ref[:, off:off + LRU_WIDTH], preferred_element_type=F32).astype(BF16)


def _inproj(x, g, w_in, tm):
    B, S, D = x.shape
    c, slo, shi = _rope_tables(S)
    row = lambda si, b: (b, si, 0)
    tab = lambda si, b: (si, 0)
    const = lambda si, b: (0, 0)
    out = jax.ShapeDtypeStruct((B, S, ATTN_WIDTH), BF16)
    w_in = w_in.astype(BF16)
    wvt = w_in[:, 2 * ATTN_WIDTH:3 * ATTN_WIDTH].T
    return pl.pallas_call(
        _inproj_kernel,
        grid=(S // tm, B),
        in_specs=[
            pl.BlockSpec((None, tm, D), row),
            pl.BlockSpec((1, D), const),
            pl.BlockSpec(w_in.shape, const),
            pl.BlockSpec(wvt.shape, const),
            pl.BlockSpec((tm, LANES), tab),
            pl.BlockSpec((tm, LANES), tab),
            pl.BlockSpec((tm, LANES), tab),
        ],
        out_specs=[pl.BlockSpec((None, tm, ATTN_WIDTH), row)] * 2
        + [pl.BlockSpec((None, ATTN_HEADS * VT_ROWS, tm), lambda si, b: (b, 0, si))]
        + [pl.BlockSpec((None, tm, LRU_WIDTH), row)] * 2,
        out_shape=[out, out, jax.ShapeDtypeStruct((B, ATTN_HEADS * VT_ROWS, S), BF16), out, out],
        compiler_params=_cparams(("arbitrary", "arbitrary")),
        name="inproj",
    )(x, g.reshape(1, D), w_in, wvt, c, slo, shi)


def _attn_kernel(lam_ref, q_ref, k_ref, vt_ref, sg_ref, o_ref, *, kc, lookahead):
    lv = lam_ref[...]
    lam = (jnp.exp(jnp.sum(lv[0:1] * lv[1:2], axis=-1, keepdims=True))
           - jnp.exp(jnp.sum(lv[2:3] * lv[3:4], axis=-1, keepdims=True)) + LAMBDA_INIT)
    q = q_ref[...]
    tq = q.shape[0]
    S = k_ref.shape[0]
    lane = lax.broadcasted_iota(jnp.int32, q.shape, 1)
    zero = jnp.zeros_like(q)
    qs = (jnp.where(lane < HEAD_DIM, q, zero), jnp.where(lane >= HEAD_DIM, q, zero))
    m = [jnp.full((1, tq), -1e30, F32) for _ in qs]
    acc = [jnp.zeros((VT_ROWS, tq), F32) for _ in qs]
    n_chunks = S // kc

    def scores(c):
        k_c = k_ref[c * kc:(c + 1) * kc, :]
        return [lax.dot_general(k_c, qm, (((1,), (1,)), ((), ())), preferred_element_type=F32) for qm in qs]

    ahead = [scores(c) for c in range(min(lookahead, n_chunks))]
    for c in range(n_chunks):
        if c + lookahead < n_chunks:
            ahead.append(scores(c + lookahead))
        vt_c = vt_ref[:, c * kc:(c + 1) * kc]
        for j, s in enumerate(ahead.pop(0)):
            m_new = jnp.maximum(m[j], jnp.max(s, axis=0, keepdims=True))
            alpha = jnp.exp2(m[j] - m_new)
            p = jnp.exp2((s - m_new).astype(BF16))
            acc[j] = alpha * acc[j] + jnp.dot(vt_c, p, preferred_element_type=F32)
            m[j] = m_new
    o = (acc[0][:V_DIM] / acc[0][V_DIM:V_DIM + 1]
         - lam * (acc[1][:V_DIM] / acc[1][V_DIM:V_DIM + 1]))
    ms = jnp.mean(o * o, axis=0, keepdims=True)
    o = (o * lax.rsqrt(ms + EPS)).T
    o_ref[...] = (o * sg_ref[...] * (1.0 - LAMBDA_INIT)).astype(o_ref.dtype)


def _attention(q, k, vt, lam_vecs, subln_g, tq, kc):
    B, S, _ = q.shape
    qmap = lambda b, h, i: (b, i, h)
    const = lambda b, h, i: (0, 0)
    return pl.pallas_call(
        functools.partial(_attn_kernel, kc=kc, lookahead=2),
        grid=(B, ATTN_HEADS, S // tq),
        in_specs=[
            pl.BlockSpec((SUBLANES, LANES), const),
            pl.BlockSpec((None, tq, V_DIM), qmap),
            pl.BlockSpec((None, S, V_DIM), lambda b, h, i: (b, 0, h)),
            pl.BlockSpec((None, VT_ROWS, S), lambda b, h, i: (b, h, 0)),
            pl.BlockSpec((1, V_DIM), const),
        ],
        out_specs=pl.BlockSpec((None, tq, V_DIM), qmap),
        out_shape=jax.ShapeDtypeStruct((B, S, ATTN_WIDTH), BF16),
        compiler_params=_cparams(("arbitrary", "arbitrary", "arbitrary")),
        name="diffattn",
    )(lam_vecs, q, k, vt, subln_g.reshape(1, V_DIM))


def _shift_rows(x, d, fill, up):
    n, c = x.shape
    if d % SUBLANES == 0:
        pad = jnp.full((d, c), fill, x.dtype)
        return jnp.concatenate([x[d:], pad]) if up else jnp.concatenate([pad, x[:n - d]])
    row = lax.broadcasted_iota(jnp.int32, (SUBLANES, c), 0)
    if up:
        r = pltpu.roll(x, n - d, 0)
        last = jnp.where(row < SUBLANES - d, r[n - SUBLANES:], fill)
        return last if n == SUBLANES else jnp.concatenate([r[:n - SUBLANES], last])
    r = pltpu.roll(x, d, 0)
    first = jnp.where(row >= d, r[:SUBLANES], fill)
    return first if n == SUBLANES else jnp.concatenate([first, r[SUBLANES:]])


def _chunk_scan(a, u, reverse):
    n = a.shape[0]
    d = 1
    while d < n:
        a_s = _shift_rows(a, d, 1.0, reverse)
        u_s = _shift_rows(u, d, 0.0, reverse)
        u = a * u_s + u
        a = a * a_s
        d *= 2
    return a, u


def _rglru_kernel(xr_ref, gate_ref, cw_ref, cb_ref, wg_ref, bg_ref, coef_ref, o_ref,
                  xp_ref, hf_ref, pf_ref, hb_ref, pb_ref, *, tcp):
    S, C = xr_ref.shape
    SEG = SUBLANES
    L = S // SEG
    blk = BF16_SUBLANES
    n = tcp * SEG
    halo_lo = 2 * SEG

    def permute_in(b, c):
        for j in range(SEG):
            src = pl.multiple_of(j * L + b * blk, blk)
            dst = (b * blk) * SEG + j
            xp_ref[pl.ds(halo_lo + dst, blk, stride=SEG), :] = xr_ref[pl.ds(src, blk), :].astype(F32)
        return c

    lax.fori_loop(0, L // blk, permute_in, 0)
    row = lax.broadcasted_iota(jnp.int32, (SEG, C), 0)
    for k in (1, 2):
        tail = xp_ref[pl.ds(halo_lo + (L - k) * SEG, SEG), :]
        xp_ref[pl.ds(halo_lo - k * SEG, SEG), :] = jnp.where(row >= 1, pltpu.roll(tail, 1, 0), 0.0)
    head = xp_ref[pl.ds(halo_lo, SEG), :]
    xp_ref[pl.ds(halo_lo + L * SEG, SEG), :] = jnp.where(row < SEG - 1, pltpu.roll(head, SEG - 1, 0), 0.0)

    cw = cw_ref[...]
    cb = cb_ref[...]
    bg = bg_ref[...]
    coef = coef_ref[...]

    def decay_and_input(r0, direction):
        w = xp_ref[pl.ds(r0, n + 3 * SEG), :]
        xc = (w[0:n] * cw[0:1] + w[SEG:SEG + n] * cw[1:2] + w[2 * SEG:2 * SEG + n] * cw[2:3]
              + w[3 * SEG:3 * SEG + n] * cw[3:4]) + cb
        lin = jnp.dot(xc.astype(BF16), wg_ref[:, 2 * C * direction:2 * C * (direction + 1)],
                      preferred_element_type=F32)
        r = jax.nn.sigmoid(lin[:, :C] + bg[2 * direction:2 * direction + 1])
        i = jax.nn.sigmoid(lin[:, C:] + bg[2 * direction + 1:2 * direction + 2])
        log_a = coef[direction:direction + 1] * r
        th = jnp.tanh(log_a)
        one_m_a2 = (-2.0 * th) * pl.reciprocal(1.0 - th, approx=True)
        root = one_m_a2 * lax.rsqrt(jnp.maximum(one_m_a2, SQRT_GUARD))
        return jnp.exp(log_a), root * (i * xc)

    n_chunks = L // tcp

    def scan_chunk(c_idx, direction, h, p, h_ref, p_ref):
        r0 = pl.multiple_of(c_idx * n, n)
        a, u = decay_and_input(r0, direction)
        hs, ps = [None] * tcp, [None] * tcp
        for s in (range(tcp) if direction == 0 else reversed(range(tcp))):
            a_s = a[s * SEG:(s + 1) * SEG]
            h = a_s * h + u[s * SEG:(s + 1) * SEG]
            p = a_s * p
            hs[s], ps[s] = h, p
        h_ref[pl.ds(r0, n), :] = jnp.concatenate(hs)
        p_ref[pl.ds(r0, n), :] = jnp.concatenate(ps)
        return h, p

    def scan_both(ci, carry):
        hf, pf, hb, pb = carry
        hf, pf = scan_chunk(ci, 0, hf, pf, hf_ref, pf_ref)
        hb, pb = scan_chunk(n_chunks - 1 - ci, 1, hb, pb, hb_ref, pb_ref)
        return hf, pf, hb, pb

    zero, one = jnp.zeros((SEG, C), F32), jnp.ones((SEG, C), F32)
    hf_end, pf_end, hb_end, pb_end = lax.fori_loop(0, n_chunks, scan_both, (zero, one, zero, one), unroll=2)
    c_f = _shift_rows(_chunk_scan(pf_end, hf_end, False)[1], 1, 0.0, False)
    c_b = _shift_rows(_chunk_scan(pb_end, hb_end, True)[1], 1, 0.0, True)

    def finish(ci, c):
        rows = pl.ds(pl.multiple_of(ci * n, n), n)
        cf = jnp.concatenate([c_f] * tcp)
        cbk = jnp.concatenate([c_b] * tcp)
        hf_ref[rows, :] = hf_ref[rows, :] + pf_ref[rows, :] * cf + hb_ref[rows, :] + pb_ref[rows, :] * cbk
        return c

    lax.fori_loop(0, L // tcp, finish, 0)

    def permute_out(b, c):
        for j in range(SEG):
            dst = pl.multiple_of(j * L + b * blk, blk)
            h = hf_ref[pl.ds((b * blk) * SEG + j, blk, stride=SEG), :]
            o_ref[pl.ds(dst, blk), :] = (h * jax.nn.gelu(gate_ref[pl.ds(dst, blk), :].astype(F32))).astype(o_ref.dtype)
        return c

    lax.fori_loop(0, L // blk, permute_out, 0)


def _block_diag(w):
    nb, d, _ = w.shape
    eye = jnp.eye(nb, dtype=w.dtype)
    return (eye[:, None, :, None] * w[:, :, None, :]).reshape(nb * d, nb * d)


def _rglru(xr, gate, conv_w, conv_b, w_r, b_r, w_i, b_i, lru_lambda, tcp):
    B, S, W = xr.shape
    C = LANES
    n_c = W // C
    mats = [_block_diag(w_r[0]), _block_diag(w_i[0]), _block_diag(w_r[1]), _block_diag(w_i[1])]
    wg = jnp.stack([jnp.concatenate([m[c * C:(c + 1) * C, c * C:(c + 1) * C] for m in mats], axis=1)
                    for c in range(n_c)]).astype(BF16)
    bg = jnp.stack([b_r[0], b_i[0], b_r[1], b_i[1]])
    coef = -LRU_C * jax.nn.softplus(-lru_lambda.astype(F32))
    blk = lambda b, c: (b, 0, c)
    par = lambda b, c: (0, c)
    return pl.pallas_call(
        functools.partial(_rglru_kernel, tcp=tcp),
        grid=(B, n_c),
        in_specs=[
            pl.BlockSpec((None, S, C), blk),
            pl.BlockSpec((None, S, C), blk),
            pl.BlockSpec((CONV_WIDTH, C), par),
            pl.BlockSpec((1, C), par),
            pl.BlockSpec((None, C, 4 * C), lambda b, c: (c, 0, 0)),
            pl.BlockSpec((4, C), par),
            pl.BlockSpec((2, C), par),
        ],
        out_specs=pl.BlockSpec((None, S, C), blk),
        out_shape=jax.ShapeDtypeStruct((B, S, W), BF16),
        scratch_shapes=[pltpu.VMEM((S + (CONV_WIDTH - 1) * SUBLANES, C), F32)] + [pltpu.VMEM((S, C), F32)] * 4,
        compiler_params=_cparams(("arbitrary", "arbitrary")),
        name="rglru",
    )(xr, gate, conv_w, conv_b.reshape(1, W), wg, bg, coef)


def _to_tiles(ref, val):
    rows = val.shape[0]
    for j in range(ROW_SUB):
        ref[pl.ds(j, rows, stride=ROW_SUB), :] = val[:, j * LANES:(j + 1) * LANES]


def _from_tiles(ref, rows):
    return jnp.concatenate([ref[pl.ds(j, rows, stride=ROW_SUB), :] for j in range(ROW_SUB)], axis=1)


def _tile_rows(r, n=1):
    start = r * ROW_SUB if isinstance(r, int) else pl.multiple_of(r * ROW_SUB, ROW_SUB)
    return pl.ds(start, n * ROW_SUB)


R_ID, R_W, R_RANK = 0, 2, 4


def _outproj_kernel(attn_ref, rnn_ref, x_ref, wa_ref, wr_ref, g_ref, wrt_ref, triu_ref,
                    x2_ref, h2_ref, route_t_ref, cnt_ref, run_ref):
    @pl.when(pl.program_id(0) == 0)
    def _():
        run_ref[...] = jnp.zeros_like(run_ref)

    mix = (jnp.dot(attn_ref[...], wa_ref[...], preferred_element_type=F32)
           + jnp.dot(rnn_ref[...], wr_ref[...], preferred_element_type=F32))
    x2 = x_ref[...] + mix
    x2_ref[...] = x2
    ms = jnp.mean(x2 * x2, axis=-1, keepdims=True)
    h2 = x2 * lax.rsqrt(ms + EPS) * g_ref[...]
    _to_tiles(h2_ref, h2)
    lt = lax.dot_general(wrt_ref[...], h2.astype(BF16), (((1,), (1,)), ((), ())),
                         preferred_element_type=F32)
    tm = lt.shape[1]
    le = lt[0:N_EXPERTS]
    lg = lt[N_EXPERTS:N_EXPERTS + SUBLANES]
    neg = jnp.float32(-jnp.inf)
    big = jnp.int32(LANES)
    grow = lax.broadcasted_iota(jnp.int32, lg.shape, 0)
    is_g = grow < N_GROUPS
    gl = jnp.where(is_g, lg, neg)
    gmax = jnp.max(gl, axis=0, keepdims=True)
    gsum = jnp.sum(jnp.where(is_g, jnp.exp(gl - gmax), 0.0), axis=0, keepdims=True)
    g_top_p = 1.0 / gsum
    g_idx = jnp.min(jnp.where(is_g & (gl == gmax), grow, big), axis=0, keepdims=True)
    erow = lax.broadcasted_iota(jnp.int32, le.shape, 0)
    e_lo = g_idx * EXPERTS_PER_GROUP
    in_grp = (erow >= e_lo) & (erow < e_lo + EXPERTS_PER_GROUP)
    el = jnp.where(in_grp, le, neg)
    e1 = jnp.max(el, axis=0, keepdims=True)
    i1 = jnp.min(jnp.where(in_grp & (el == e1), erow, big), axis=0, keepdims=True)
    el2 = jnp.where(erow == i1, neg, el)
    e2 = jnp.max(el2, axis=0, keepdims=True)
    i2 = jnp.min(jnp.where(in_grp & (erow != i1) & (el2 == e2), erow, big), axis=0, keepdims=True)
    t = jnp.exp(e2 - e1)
    w1 = g_top_p / (1.0 + t)
    w2 = g_top_p * t / (1.0 + t)
    oh1 = erow == i1
    oh2 = erow == i2
    oh = jnp.where(oh1 | oh2, 1.0, 0.0)
    run = run_ref[:, 0:1]
    before = jnp.dot(oh.astype(BF16), triu_ref[...], preferred_element_type=F32) + run
    rank1 = jnp.sum(jnp.where(oh1, before, 0.0), axis=0, keepdims=True)
    rank2 = jnp.sum(jnp.where(oh2, before, 0.0), axis=0, keepdims=True)
    run_ref[...] = run_ref[...] + jnp.sum(oh, axis=1, keepdims=True)
    cnt_ref[...] = run_ref[...]
    fields = [None] * SUBLANES
    fields[R_ID], fields[R_ID + 1] = i1.astype(F32), i2.astype(F32)
    fields[R_W], fields[R_W + 1] = w1, w2
    fields[R_RANK], fields[R_RANK + 1] = rank1, rank2
    zero = jnp.zeros((1, tm), F32)
    route_t_ref[...] = jnp.concatenate([zero if f is None else f for f in fields], axis=0)


def _outproj_router(attn, rnn, x, w_out, g2, w_grp, w_exp, tm):
    N, D = x.shape
    w_out = w_out.astype(BF16)
    wrt = jnp.zeros((N_EXPERTS + SUBLANES, D), F32).at[:N_EXPERTS].set(w_exp.T)
    wrt = wrt.at[N_EXPERTS:N_EXPERTS + N_GROUPS].set(w_grp.T).astype(BF16)
    triu = jnp.triu(jnp.ones((tm, tm), BF16), 1)
    row = lambda i: (i, 0)
    const = lambda i: (0, 0)
    return pl.pallas_call(
        _outproj_kernel,
        grid=(N // tm,),
        in_specs=[
            pl.BlockSpec((tm, ATTN_WIDTH), row),
            pl.BlockSpec((tm, LRU_WIDTH), row),
            pl.BlockSpec((tm, D), row),
            pl.BlockSpec((ATTN_WIDTH, D), const),
            pl.BlockSpec((LRU_WIDTH, D), const),
            pl.BlockSpec((1, D), const),
            pl.BlockSpec(wrt.shape, const),
            pl.BlockSpec((tm, tm), const),
        ],
        out_specs=[pl.BlockSpec((tm, D), row), pl.BlockSpec((tm * ROW_SUB, LANES), row),
                   pl.BlockSpec((SUBLANES, tm), lambda i: (0, i)), pl.BlockSpec((N_EXPERTS, LANES), const)],
        out_shape=[jax.ShapeDtypeStruct((N, D), F32), jax.ShapeDtypeStruct((N * ROW_SUB, LANES), F32),
                   jax.ShapeDtypeStruct((SUBLANES, N), F32), jax.ShapeDtypeStruct((N_EXPERTS, LANES), F32)],
        scratch_shapes=[pltpu.VMEM((N_EXPERTS, LANES), F32)],
        compiler_params=_cparams(("arbitrary",)),
        name="outproj_router",
    )(attn, rnn, x, w_out[:ATTN_WIDTH], w_out[ATTN_WIDTH:], g2.reshape(1, D), wrt, triu)


def _dispatch_plan(route_t, counts, tg):
    N = route_t.shape[1]
    ids = route_t[R_ID:R_ID + 2].astype(jnp.int32)
    rank = route_t[R_RANK:R_RANK + 2].astype(jnp.int32)
    cnt = counts[:, 0].astype(jnp.int32)
    tiles_e = (cnt + tg - 1) // tg
    tile_end = jnp.cumsum(tiles_e)
    row_off = (tile_end - tiles_e) * tg
    pos = rank
    for e in range(N_EXPERTS):
        pos = pos + jnp.where(ids == e, row_off[e], 0)
    n_tiles = (2 * N) // tg + N_EXPERTS
    n_act = tile_end[-1]
    tile_id = jnp.minimum(jnp.arange(n_tiles, dtype=jnp.int32), n_act - 1)
    tile_expert = jnp.sum((tile_id[:, None] >= tile_end[None, :]).astype(jnp.int32), axis=1)
    pad_start = row_off + cnt
    pad_n = tiles_e * tg - cnt
    return pos[0], pos[1], tile_expert, n_act.reshape(1).astype(jnp.int32), pad_start, pad_n, n_tiles


def _row_copy(src, src_row, dst, dst_row, sem):
    return pltpu.make_async_copy(src.at[_tile_rows(src_row)], dst.at[_tile_rows(dst_row)], sem)


def _scatter_kernel(pad_start_ref, pad_n_ref, nact_ref, pos0_ref, pos1_ref, h_ref, xs_ref,
                    zero_ref, stage_ref, sem, sems, *, tg):
    tm = h_ref.shape[0] // ROW_SUB
    half = tg // 2

    @pl.when(pl.program_id(0) == 0)
    def _():
        zero_ref[...] = jnp.zeros_like(zero_ref)

        def fill_tile(j, c):
            cps = [pltpu.make_async_copy(zero_ref, xs_ref.at[_tile_rows(j * tg + h * half, half)], sem)
                   for h in range(2)]
            for cp in cps:
                cp.start()
            for cp in cps:
                cp.wait()
            return c

        lax.fori_loop(nact_ref[0], xs_ref.shape[0] // (tg * ROW_SUB), fill_tile, 0)

    @pl.when(pl.program_id(0) == 0)
    def _():
        for e in range(N_EXPERTS):
            n = pad_n_ref[e]
            off = pad_start_ref[e]
            b = 1
            while b < tg:
                hit = (n & b) != 0

                @pl.when(hit)
                def _(off=off, b=b):
                    cp = pltpu.make_async_copy(zero_ref.at[_tile_rows(0, b)], xs_ref.at[_tile_rows(off, b)], sem)
                    cp.start()
                    cp.wait()

                off = off + jnp.where(hit, b, 0)
                b *= 2

    step = pl.program_id(0)
    slot = step % 2
    stage = stage_ref.at[slot]
    stage[...] = h_ref[...]

    def issue(r, c):
        _row_copy(stage, r, xs_ref, pos0_ref[r], sems.at[slot]).start(priority=0)
        _row_copy(stage, r, xs_ref, pos1_ref[r], sems.at[slot]).start(priority=1)
        return c

    lax.fori_loop(0, tm, issue, 0, unroll=8)

    def drain(which):
        def body(r, c):
            _row_copy(stage, 0, xs_ref, 0, sems.at[which]).wait()
            _row_copy(stage, 0, xs_ref, 0, sems.at[which]).wait()
            return c

        lax.fori_loop(0, tm, body, 0, unroll=8)

    @pl.when(step > 0)
    def _():
        drain(1 - slot)

    @pl.when(step == pl.num_programs(0) - 1)
    def _():
        drain(slot)


def _scatter_rows(h2, pos0, pos1, pad_start, pad_n, n_act, n_rows, tm, tg):
    N = h2.shape[0] // ROW_SUB
    tok = lambda i, ps, pn, na: (i,)
    return pl.pallas_call(
        functools.partial(_scatter_kernel, tg=tg),
        grid_spec=pltpu.PrefetchScalarGridSpec(
            num_scalar_prefetch=3,
            grid=(N // tm,),
            in_specs=[
                pl.BlockSpec((tm,), tok, memory_space=pltpu.SMEM),
                pl.BlockSpec((tm,), tok, memory_space=pltpu.SMEM),
                pl.BlockSpec((tm * ROW_SUB, LANES), lambda i, ps, pn, na: (i, 0)),
            ],
            out_specs=pl.BlockSpec(memory_space=pl.ANY),
            scratch_shapes=[pltpu.VMEM((tg // 2 * ROW_SUB, LANES), F32), pltpu.VMEM((2, tm * ROW_SUB, LANES), F32),
                            pltpu.SemaphoreType.DMA, pltpu.SemaphoreType.DMA((2,))],
        ),
        out_shape=jax.ShapeDtypeStruct((n_rows * ROW_SUB, LANES), F32),
        compiler_params=_cparams(("arbitrary",)),
        name="moe_scatter",
    )(pad_start, pad_n, n_act, pos0, pos1, h2)


def _experts_kernel(te_ref, nact_ref, x_ref, wg_ref, wu_ref, wd_ref, y_ref, wg_s, wu_s, wd_s):
    j = pl.program_id(0)
    active = j < nact_ref[0]
    new_expert = (j == 0) | (te_ref[j] != te_ref[jnp.maximum(j - 1, 0)])

    @pl.when(active & new_expert)
    def _():
        wg_s[...] = wg_ref[...].astype(BF16)
        wu_s[...] = wu_ref[...].astype(BF16)
        wd_s[...] = wd_ref[...].astype(BF16)

    @pl.when(active)
    def _():
        x = _from_tiles(x_ref, x_ref.shape[0] // ROW_SUB).astype(BF16)
        gt = jnp.dot(x, wg_s[...], preferred_element_type=F32)
        up = jnp.dot(x, wu_s[...], preferred_element_type=F32)
        hid = (gt * jax.nn.sigmoid(gt)) * up
        _to_tiles(y_ref, jnp.dot(hid.astype(BF16), wd_s[...], preferred_element_type=F32))

    @pl.when(jnp.logical_not(active))
    def _():
        y_ref[...] = jnp.zeros_like(y_ref)


def _experts(xs, tile_expert, n_act, w_gate, w_up, w_down, tg):
    R, D = xs.shape[0] // ROW_SUB, D_MODEL
    rows = lambda j, te, na: (j, 0)
    wmap = lambda j, te, na: (te[j], 0, 0)
    return pl.pallas_call(
        _experts_kernel,
        grid_spec=pltpu.PrefetchScalarGridSpec(
            num_scalar_prefetch=2,
            grid=(R // tg,),
            in_specs=[
                pl.BlockSpec((tg * ROW_SUB, LANES), rows),
                pl.BlockSpec((None, D, D_EXPERT), wmap),
                pl.BlockSpec((None, D, D_EXPERT), wmap),
                pl.BlockSpec((None, D_EXPERT, D), wmap),
            ],
            out_specs=pl.BlockSpec((tg * ROW_SUB, LANES), rows),
            scratch_shapes=[pltpu.VMEM((D, D_EXPERT), BF16), pltpu.VMEM((D, D_EXPERT), BF16),
                            pltpu.VMEM((D_EXPERT, D), BF16)],
        ),
        out_shape=jax.ShapeDtypeStruct((R * ROW_SUB, LANES), F32),
        compiler_params=_cparams(("arbitrary",)),
        name="moe_experts",
    )(tile_expert, n_act, xs, w_gate, w_up, w_down)


def _combine_kernel(pos0_ref, pos1_ref, pos0n_ref, pos1n_ref, route_ref, x2_ref, fg_ref, ys_ref, o_ref,
                    buf_ref, sems):
    tm = x2_ref.shape[0]
    step = pl.program_id(0)
    slot = step % 2

    def issue(p0_ref, p1_ref, which):
        def body(r, c):
            _row_copy(ys_ref, p0_ref[r], buf_ref.at[which, 0], r, sems.at[which]).start(priority=0)
            _row_copy(ys_ref, p1_ref[r], buf_ref.at[which, 1], r, sems.at[which]).start(priority=1)
            return c

        lax.fori_loop(0, tm, body, 0, unroll=8)

    @pl.when(step == 0)
    def _():
        issue(pos0_ref, pos1_ref, slot)

    @pl.when(step + 1 < pl.num_programs(0))
    def _():
        issue(pos0n_ref, pos1n_ref, 1 - slot)

    def drain(r, c):
        _row_copy(ys_ref, 0, buf_ref.at[slot, 0], 0, sems.at[slot]).wait()
        _row_copy(ys_ref, 0, buf_ref.at[slot, 0], 0, sems.at[slot]).wait()
        return c

    lax.fori_loop(0, tm, drain, 0, unroll=8)
    route = route_ref[...].T
    w1 = route[:, R_W:R_W + 1]
    w2 = route[:, R_W + 1:R_W + 2]
    y = x2_ref[...] + w1 * _from_tiles(buf_ref.at[slot, 0], tm) + w2 * _from_tiles(buf_ref.at[slot, 1], tm)
    ms = jnp.mean(y * y, axis=-1, keepdims=True)
    o_ref[...] = y * lax.rsqrt(ms + EPS) * fg_ref[...]


def _combine(ys, pos0, pos1, route, x2, final_g, tm):
    N, D = x2.shape
    n = N // tm
    row = lambda i: (i, 0)
    cur = lambda i: (i,)
    nxt = lambda i: (jnp.minimum(i + 1, n - 1),)
    smem = functools.partial(pl.BlockSpec, (tm,), memory_space=pltpu.SMEM)
    return pl.pallas_call(
        _combine_kernel,
        grid=(n,),
        in_specs=[
            smem(cur), smem(cur), smem(nxt), smem(nxt),
            pl.BlockSpec((SUBLANES, tm), lambda i: (0, i)),
            pl.BlockSpec((tm, D), row),
            pl.BlockSpec((1, D), lambda i: (0, 0)),
            pl.BlockSpec(memory_space=pl.ANY),
        ],
        out_specs=pl.BlockSpec((tm, D), row),
        out_shape=jax.ShapeDtypeStruct((N, D), F32),
        scratch_shapes=[pltpu.VMEM((2, 2, tm * ROW_SUB, LANES), F32), pltpu.SemaphoreType.DMA((2,))],
        compiler_params=_cparams(("arbitrary",)),
        name="moe_combine",
    )(pos0, pos1, pos0, pos1, route, x2, final_g.reshape(1, D), ys)


def _tiles(B, S):
    return dict(
        tm_in=min(512, S),
        tq=min(2048, S),
        kc=min(256, S),
        tcp=min(32, S // 16),
        tm_out=min(512, S),
        tm_row=min(512, S),
        tg=min(512, S),
    )


def kernel(x, norm1_g, w_in, lambda_q1, lambda_k1, lambda_q2, lambda_k2, subln_g, conv_w, conv_b,
           lru_w_r, lru_b_r, lru_w_i, lru_b_i, lru_lambda, w_out, norm2_g, w_grp, w_exp, w_gate, w_up,
           w_down, final_g):
    B, S, D = x.shape
    t = _tiles(B, S)
    l = 0
    q, k, vt, xr, gate = _inproj(x, norm1_g[l], w_in[l], t["tm_in"])
    lam_vecs = jnp.zeros((SUBLANES, LANES), F32).at[0:4, :HEAD_DIM].set(
        jnp.stack([lambda_q1[l], lambda_k1[l], lambda_q2[l], lambda_k2[l]]))
    attn = _attention(q, k, vt, lam_vecs, subln_g[l], t["tq"], t["kc"])
    rnn = _rglru(xr, gate, conv_w[l], conv_b[l], lru_w_r[l], lru_b_r[l], lru_w_i[l], lru_b_i[l],
                 lru_lambda[l], t["tcp"])
    N = B * S
    x2, h2, route_t, counts = _outproj_router(
        attn.reshape(N, ATTN_WIDTH), rnn.reshape(N, LRU_WIDTH), x.reshape(N, D), w_out[l], norm2_g[l],
        w_grp[l], w_exp[l], t["tm_out"])
    tg = t["tg"]
    pos0, pos1, tile_expert, n_act, pad_start, pad_n, n_tiles = _dispatch_plan(route_t, counts, tg)
    xs = _scatter_rows(h2, pos0, pos1, pad_start, pad_n, n_act, n_tiles * tg, t["tm_row"], tg)
    ys = _experts(xs, tile_expert, n_act, w_gate[l], w_up[l], w_down[l], tg)
    out = _combine(ys, pos0, pos1, route_t, x2, final_g, t["tm_row"])
    return out.reshape(B, S, D)
```

```python
import functools
import math

import jax
import jax.numpy as jnp
from jax import lax
from jax.experimental import pallas as pl
from jax.experimental.pallas import tpu as pltpu

F32 = jnp.float32
BF16 = jnp.bfloat16

D_MODEL = 1024
ATTN_HEADS = 4
HEAD_DIM = 64
V_DIM = 2 * HEAD_DIM
ATTN_WIDTH = ATTN_HEADS * V_DIM
ROPE_THETA = 500000.0
ROPE_DIM = HEAD_DIM // 4
LRU_WIDTH = D_MODEL - ATTN_WIDTH
LRU_BLOCK_DIM = 64
CONV_WIDTH = 4
LRU_C = 8.0
N_GROUPS = 4
EXPERTS_PER_GROUP = 8
N_EXPERTS = N_GROUPS * EXPERTS_PER_GROUP
D_EXPERT = 256
EPS = 1e-6
SQRT_GUARD = 1e-37
LAMBDA_INIT = 0.8 - 0.6 * math.exp(0.0)

LANES = 128
SUBLANES = 8
MXU_DIM = 256
BF16_SUBLANES = 16
VT_ROWS = V_DIM + BF16_SUBLANES
ROW_SUB = D_MODEL // LANES
assert ROW_SUB == SUBLANES
VMEM_LIMIT = 56 * 1024 * 1024


def _cparams(semantics, flags=None):
    return pltpu.CompilerParams(dimension_semantics=semantics, vmem_limit_bytes=VMEM_LIMIT, flags=flags)


def _rope_tables(seq_len):
    pos = jnp.arange(seq_len, dtype=F32)
    inv_freq = ROPE_THETA ** (-jnp.arange(0, ROPE_DIM, 2, dtype=F32) / ROPE_DIM)
    ang = pos[:, None] * inv_freq[None, :]
    cos, sin = jnp.cos(ang), jnp.sin(ang)
    half = ROPE_DIM // 2
    j = jnp.arange(LANES) % HEAD_DIM
    f = j % half
    c = jnp.where(j[None, :] < ROPE_DIM, cos[:, f], 1.0)
    s_lo = jnp.where(j[None, :] < half, -sin[:, f], 0.0)
    s_hi = jnp.where((j[None, :] >= half) & (j[None, :] < ROPE_DIM), sin[:, f], 0.0)
    return c, s_lo, s_hi


def _inproj_kernel(x_ref, g_ref, w_ref, wvt_ref, c_ref, slo_ref, shi_ref,
                   q_ref, k_ref, vt_ref, xr_ref, gate_ref):
    x = x_ref[...]
    ms = jnp.mean(x * x, axis=-1, keepdims=True)
    h = (x * lax.rsqrt(ms + EPS) * g_ref[...]).astype(BF16)
    c, slo, shi = c_ref[...], slo_ref[...], shi_ref[...]

    def rope(p):
        return p * c + pltpu.roll(p, LANES - ROPE_DIM // 2, 1) * slo + pltpu.roll(p, ROPE_DIM // 2, 1) * shi

    q_scale = HEAD_DIM ** -0.5 * math.log2(math.e)
    pq = jnp.dot(h, w_ref[:, 0:ATTN_WIDTH], preferred_element_type=F32)
    pk = jnp.dot(h, w_ref[:, ATTN_WIDTH:2 * ATTN_WIDTH], preferred_element_type=F32)
    for hd in range(ATTN_HEADS):
        lo = hd * V_DIM
        q_ref[:, lo:lo + V_DIM] = (rope(pq[:, lo:lo + V_DIM]) * q_scale).astype(BF16)
        k_ref[:, lo:lo + V_DIM] = rope(pk[:, lo:lo + V_DIM]).astype(BF16)
    vt = lax.dot_general(wvt_ref[...], h, (((1,), (1,)), ((), ())), preferred_element_type=F32).astype(BF16)
    for hd in range(ATTN_HEADS):
        vt_ref[hd * VT_ROWS:hd * VT_ROWS + V_DIM, :] = vt[hd * V_DIM:(hd + 1) * V_DIM]
        vt_ref[hd * VT_ROWS + V_DIM:(hd + 1) * VT_ROWS, :] = jnp.ones((VT_ROWS - V_DIM, vt.shape[1]), BF16)
    off = 3 * ATTN_WIDTH
    xr_ref[...] = jnp.dot(h, w_ref[:, off:off + LRU_WIDTH], preferred_element_type=F32).astype(BF16)
    off += LRU_WIDTH
    gate_ref[...] = jnp.dot(h, w_ref[:, off:off + LRU_WIDTH], preferred_element_type=F32).astype(BF16)


def _inproj(x, g, w_in, tm):
    B, S, D = x.shape
    c, slo, shi = _rope_tables(S)
    row = lambda si, b: (b, si, 0)
    tab = lambda si, b: (si, 0)
    const = lambda si, b: (0, 0)
    out = jax.ShapeDtypeStruct((B, S, ATTN_WIDTH), BF16)
    w_in = w_in.astype(BF16)
    wvt = w_in[:, 2 * ATTN_WIDTH:3 * ATTN_WIDTH].T
    return pl.pallas_call(
        _inproj_kernel,
        grid=(S // tm, B),
        in_specs=[
            pl.BlockSpec((None, tm, D), row),
            pl.BlockSpec((1, D), const),
            pl.BlockSpec(w_in.shape, const),
            pl.BlockSpec(wvt.shape, const),
            pl.BlockSpec((tm, LANES), tab),
            pl.BlockSpec((tm, LANES), tab),
            pl.BlockSpec((tm, LANES), tab),
        ],
        out_specs=[pl.BlockSpec((None, tm, ATTN_WIDTH), row)] * 2
        + [pl.BlockSpec((None, ATTN_HEADS * VT_ROWS, tm), lambda si, b: (b, 0, si))]
        + [pl.BlockSpec((None, tm, LRU_WIDTH), row)] * 2,
        out_shape=[out, out, jax.ShapeDtypeStruct((B, ATTN_HEADS * VT_ROWS, S), BF16), out, out],
        compiler_params=_cparams(("arbitrary", "arbitrary")),
        name="inproj",
    )(x, g.reshape(1, D), w_in, wvt, c, slo, shi)


def _attn_kernel(lam_ref, q_ref, k_ref, vt_ref, sg_ref, o_ref, *, kc, lookahead):
    lv = lam_ref[...]
    lam = (jnp.exp(jnp.sum(lv[0:1] * lv[1:2], axis=-1, keepdims=True))
           - jnp.exp(jnp.sum(lv[2:3] * lv[3:4], axis=-1, keepdims=True)) + LAMBDA_INIT)
    q = q_ref[...]
    tq = q.shape[0]
    S = k_ref.shape[0]
    lane = lax.broadcasted_iota(jnp.int32, q.shape, 1)
    zero = jnp.zeros_like(q)
    qs = (jnp.where(lane < HEAD_DIM, q, zero), jnp.where(lane >= HEAD_DIM, q, zero))
    m = [jnp.full((1, tq), -1e30, F32) for _ in qs]
    acc = [jnp.zeros((VT_ROWS, tq), F32) for _ in qs]
    n_chunks = S // kc

    def scores(c):
        k_c = k_ref[c * kc:(c + 1) * kc, :]
        return [lax.dot_general(k_c, qm, (((1,), (1,)), ((), ())), preferred_element_type=F32) for qm in qs]

    ahead = [scores(c) for c in range(min(lookahead, n_chunks))]
    for c in range(n_chunks):
        if c + lookahead < n_chunks:
            ahead.append(scores(c + lookahead))
        vt_c = vt_ref[:, c * kc:(c + 1) * kc]
        for j, s in enumerate(ahead.pop(0)):
            m_new = jnp.maximum(m[j], jnp.max(s, axis=0, keepdims=True))
            alpha = jnp.exp2(m[j] - m_new)
            p = jnp.exp2((s - m_new).astype(BF16))
            acc[j] = alpha * acc[j] + jnp.dot(vt_c, p, preferred_element_type=F32)
            m[j] = m_new
    o = (acc[0][:V_DIM] / acc[0][V_DIM:V_DIM + 1]
         - lam * (acc[1][:V_DIM] / acc[1][V_DIM:V_DIM + 1]))
    ms = jnp.mean(o * o, axis=0, keepdims=True)
    o = (o * lax.rsqrt(ms + EPS)).T
    o_ref[...] = (o * sg_ref[...] * (1.0 - LAMBDA_INIT)).astype(o_ref.dtype)


def _attention(q, k, vt, lam_vecs, subln_g, tq, kc):
    B, S, _ = q.shape
    qmap = lambda b, h, i: (b, i, h)
    const = lambda b, h, i: (0, 0)
    return pl.pallas_call(
        functools.partial(_attn_kernel, kc=kc, lookahead=2),
        grid=(B, ATTN_HEADS, S // tq),
        in_specs=[
            pl.BlockSpec((SUBLANES, LANES), const),
            pl.BlockSpec((None, tq, V_DIM), qmap),
            pl.BlockSpec((None, S, V_DIM), lambda b, h, i: (b, 0, h)),
            pl.BlockSpec((None, VT_ROWS, S), lambda b, h, i: (b, h, 0)),
            pl.BlockSpec((1, V_DIM), const),
        ],
        out_specs=pl.BlockSpec((None, tq, V_DIM), qmap),
        out_shape=jax.ShapeDtypeStruct((B, S, ATTN_WIDTH), BF16),
        compiler_params=_cparams(("arbitrary", "arbitrary", "arbitrary")),
        name="diffattn",
    )(lam_vecs, q, k, vt, subln_g.reshape(1, V_DIM))


def _shift_rows(x, d, fill, up):
    n, c = x.shape
    if d % SUBLANES == 0:
        pad = jnp.full((d, c), fill, x.dtype)
        return jnp.concatenate([x[d:], pad]) if up else jnp.concatenate([pad, x[:n - d]])
    row = lax.broadcasted_iota(jnp.int32, (SUBLANES, c), 0)
    if up:
        r = pltpu.roll(x, n - d, 0)
        last = jnp.where(row < SUBLANES - d, r[n - SUBLANES:], fill)
        return last if n == SUBLANES else jnp.concatenate([r[:n - SUBLANES], last])
    r = pltpu.roll(x, d, 0)
    first = jnp.where(row >= d, r[:SUBLANES], fill)
    return first if n == SUBLANES else jnp.concatenate([first, r[SUBLANES:]])


def _chunk_scan(a, u, reverse):
    n = a.shape[0]
    d = 1
    while d < n:
        a_s = _shift_rows(a, d, 1.0, reverse)
        u_s = _shift_rows(u, d, 0.0, reverse)
        u = a * u_s + u
        a = a * a_s
        d *= 2
    return a, u


def _rglru_kernel(xr_ref, gate_ref, cw_ref, cb_ref, wg_ref, bg_ref, coef_ref, o_ref,
                  xp_ref, hf_ref, pf_ref, hb_ref, pb_ref, *, tcp):
    S, C = xr_ref.shape
    SEG = SUBLANES
    L = S // SEG
    blk = BF16_SUBLANES
    n = tcp * SEG
    halo_lo = 2 * SEG

    def permute_in(b, c):
        for j in range(SEG):
            src = pl.multiple_of(j * L + b * blk, blk)
            dst = (b * blk) * SEG + j
            xp_ref[pl.ds(halo_lo + dst, blk, stride=SEG), :] = xr_ref[pl.ds(src, blk), :].astype(F32)
        return c

    lax.fori_loop(0, L // blk, permute_in, 0)
    row = lax.broadcasted_iota(jnp.int32, (SEG, C), 0)
    for k in (1, 2):
        tail = xp_ref[pl.ds(halo_lo + (L - k) * SEG, SEG), :]
        xp_ref[pl.ds(halo_lo - k * SEG, SEG), :] = jnp.where(row >= 1, pltpu.roll(tail, 1, 0), 0.0)
    head = xp_ref[pl.ds(halo_lo, SEG), :]
    xp_ref[pl.ds(halo_lo + L * SEG, SEG), :] = jnp.where(row < SEG - 1, pltpu.roll(head, SEG - 1, 0), 0.0)

    cw = cw_ref[...]
    cb = cb_ref[...]
    bg = bg_ref[...]
    coef = coef_ref[...]

    def decay_and_input(r0, direction):
        w = xp_ref[pl.ds(r0, n + 3 * SEG), :]
        xc = (w[0:n] * cw[0:1] + w[SEG:SEG + n] * cw[1:2] + w[2 * SEG:2 * SEG + n] * cw[2:3]
              + w[3 * SEG:3 * SEG + n] * cw[3:4]) + cb
        lin = jnp.dot(xc.astype(BF16), wg_ref[:, 2 * C * direction:2 * C * (direction + 1)],
                      preferred_element_type=F32)
        r = jax.nn.sigmoid(lin[:, :C] + bg[2 * direction:2 * direction + 1])
        i = jax.nn.sigmoid(lin[:, C:] + bg[2 * direction + 1:2 * direction + 2])
        log_a = coef[direction:direction + 1] * r
        th = jnp.tanh(log_a)
        one_m_a2 = (-2.0 * th) * pl.reciprocal(1.0 - th, approx=True)
        root = one_m_a2 * lax.rsqrt(jnp.maximum(one_m_a2, SQRT_GUARD))
        return jnp.exp(log_a), root * (i * xc)

    n_chunks = L // tcp

    def scan_chunk(c_idx, direction, h, p, h_ref, p_ref):
        r0 = pl.multiple_of(c_idx * n, n)
        a, u = decay_and_input(r0, direction)
        hs, ps = [None] * tcp, [None] * tcp
        for s in (range(tcp) if direction == 0 else reversed(range(tcp))):
            a_s = a[s * SEG:(s + 1) * SEG]
            h = a_s * h + u[s * SEG:(s + 1) * SEG]
            p = a_s * p
            hs[s], ps[s] = h, p
        h_ref[pl.ds(r0, n), :] = jnp.concatenate(hs)
        p_ref[pl.ds(r0, n), :] = jnp.concatenate(ps)
        return h, p

    def scan_both(ci, carry):
        hf, pf, hb, pb = carry
        hf, pf = scan_chunk(ci, 0, hf, pf, hf_ref, pf_ref)
        hb, pb = scan_chunk(n_chunks - 1 - ci, 1, hb, pb, hb_ref, pb_ref)
        return hf, pf, hb, pb

    zero, one = jnp.zeros((SEG, C), F32), jnp.ones((SEG, C), F32)
    hf_end, pf_end, hb_end, pb_end = lax.fori_loop(0, n_chunks, scan_both, (zero, one, zero, one), unroll=2)
    c_f = _shift_rows(_chunk_scan(pf_end, hf_end, False)[1], 1, 0.0, False)
    c_b = _shift_rows(_chunk_scan(pb_end, hb_end, True)[1], 1, 0.0, True)

    def finish(ci, c):
        rows = pl.ds(pl.multiple_of(ci * n, n), n)
        cf = jnp.concatenate([c_f] * tcp)
        cbk = jnp.concatenate([c_b] * tcp)
        hf_ref[rows, :] = hf_ref[rows, :] + pf_ref[rows, :] * cf + hb_ref[rows, :] + pb_ref[rows, :] * cbk
        return c

    lax.fori_loop(0, L // tcp, finish, 0)

    def permute_out(b, c):
        for j in range(SEG):
            dst = pl.multiple_of(j * L + b * blk, blk)
            h = hf_ref[pl.ds((b * blk) * SEG + j, blk, stride=SEG), :]
            o_ref[pl.ds(dst, blk), :] = (h * jax.nn.gelu(gate_ref[pl.ds(dst, blk), :].astype(F32))).astype(o_ref.dtype)
        return c

    lax.fori_loop(0, L // blk, permute_out, 0)


def _block_diag(w):
    nb, d, _ = w.shape
    eye = jnp.eye(nb, dtype=w.dtype)
    return (eye[:, None, :, None] * w[:, :, None, :]).reshape(nb * d, nb * d)


def _rglru(xr, gate, conv_w, conv_b, w_r, b_r, w_i, b_i, lru_lambda, tcp):
    B, S, W = xr.shape
    C = LANES
    n_c = W // C
    mats = [_block_diag(w_r[0]), _block_diag(w_i[0]), _block_diag(w_r[1]), _block_diag(w_i[1])]
    wg = jnp.stack([jnp.concatenate([m[c * C:(c + 1) * C, c * C:(c + 1) * C] for m in mats], axis=1)
                    for c in range(n_c)]).astype(BF16)
    bg = jnp.stack([b_r[0], b_i[0], b_r[1], b_i[1]])
    coef = -LRU_C * jax.nn.softplus(-lru_lambda.astype(F32))
    blk = lambda b, c: (b, 0, c)
    par = lambda b, c: (0, c)
    return pl.pallas_call(
        functools.partial(_rglru_kernel, tcp=tcp),
        grid=(B, n_c),
        in_specs=[
            pl.BlockSpec((None, S, C), blk),
            pl.BlockSpec((None, S, C), blk),
            pl.BlockSpec((CONV_WIDTH, C), par),
            pl.BlockSpec((1, C), par),
            pl.BlockSpec((None, C, 4 * C), lambda b, c: (c, 0, 0)),
            pl.BlockSpec((4, C), par),
            pl.BlockSpec((2, C), par),
        ],
        out_specs=pl.BlockSpec((None, S, C), blk),
        out_shape=jax.ShapeDtypeStruct((B, S, W), BF16),
        scratch_shapes=[pltpu.VMEM((S + (CONV_WIDTH - 1) * SUBLANES, C), F32)] + [pltpu.VMEM((S, C), F32)] * 4,
        compiler_params=_cparams(("arbitrary", "arbitrary")),
        name="rglru",
    )(xr, gate, conv_w, conv_b.reshape(1, W), wg, bg, coef)


def _to_tiles(ref, val):
    rows = val.shape[0]
    for j in range(ROW_SUB):
        ref[pl.ds(j, rows, stride=ROW_SUB), :] = val[:, j * LANES:(j + 1) * LANES]


def _from_tiles(ref, rows):
    return jnp.concatenate([ref[pl.ds(j, rows, stride=ROW_SUB), :] for j in range(ROW_SUB)], axis=1)


def _tile_rows(r, n=1):
    start = r * ROW_SUB if isinstance(r, int) else pl.multiple_of(r * ROW_SUB, ROW_SUB)
    return pl.ds(start, n * ROW_SUB)


R_ID, R_W, R_RANK = 0, 2, 4


def _outproj_kernel(attn_ref, rnn_ref, x_ref, wa_ref, wr_ref, g_ref, wrt_ref, triu_ref,
                    x2_ref, h2_ref, route_t_ref, cnt_ref, run_ref):
    @pl.when(pl.program_id(0) == 0)
    def _():
        run_ref[...] = jnp.zeros_like(run_ref)

    mix = (jnp.dot(attn_ref[...], wa_ref[...], preferred_element_type=F32)
           + jnp.dot(rnn_ref[...], wr_ref[...], preferred_element_type=F32))
    x2 = x_ref[...] + mix
    x2_ref[...] = x2
    ms = jnp.mean(x2 * x2, axis=-1, keepdims=True)
    h2 = x2 * lax.rsqrt(ms + EPS) * g_ref[...]
    _to_tiles(h2_ref, h2)
    lt = lax.dot_general(wrt_ref[...], h2.astype(BF16), (((1,), (1,)), ((), ())),
                         preferred_element_type=F32)
    tm = lt.shape[1]
    le = lt[0:N_EXPERTS]
    lg = lt[N_EXPERTS:N_EXPERTS + SUBLANES]
    neg = jnp.float32(-jnp.inf)
    big = jnp.int32(LANES)
    grow = lax.broadcasted_iota(jnp.int32, lg.shape, 0)
    is_g = grow < N_GROUPS
    gl = jnp.where(is_g, lg, neg)
    gmax = jnp.max(gl, axis=0, keepdims=True)
    gsum = jnp.sum(jnp.where(is_g, jnp.exp(gl - gmax), 0.0), axis=0, keepdims=True)
    g_top_p = 1.0 / gsum
    g_idx = jnp.min(jnp.where(is_g & (gl == gmax), grow, big), axis=0, keepdims=True)
    erow = lax.broadcasted_iota(jnp.int32, le.shape, 0)
    e_lo = g_idx * EXPERTS_PER_GROUP
    in_grp = (erow >= e_lo) & (erow < e_lo + EXPERTS_PER_GROUP)
    el = jnp.where(in_grp, le, neg)
    e1 = jnp.max(el, axis=0, keepdims=True)
    i1 = jnp.min(jnp.where(in_grp & (el == e1), erow, big), axis=0, keepdims=True)
    el2 = jnp.where(erow == i1, neg, el)
    e2 = jnp.max(el2, axis=0, keepdims=True)
    i2 = jnp.min(jnp.where(in_grp & (erow != i1) & (el2 == e2), erow, big), axis=0, keepdims=True)
    t = jnp.exp(e2 - e1)
    w1 = g_top_p / (1.0 + t)
    w2 = g_top_p * t / (1.0 + t)
    oh1 = erow == i1
    oh2 = erow == i2
    oh = jnp.where(oh1 | oh2, 1.0, 0.0)
    run = run_ref[:, 0:1]
    before = jnp.dot(oh.astype(BF16), triu_ref[...], preferred_element_type=F32) + run
    rank1 = jnp.sum(jnp.where(oh1, before, 0.0), axis=0, keepdims=True)
    rank2 = jnp.sum(jnp.where(oh2, before, 0.0), axis=0, keepdims=True)
    run_ref[...] = run_ref[...] + jnp.sum(oh, axis=1, keepdims=True)
    cnt_ref[...] = run_ref[...]
    fields = [None] * SUBLANES
    fields[R_ID], fields[R_ID + 1] = i1.astype(F32), i2.astype(F32)
    fields[R_W], fields[R_W + 1] = w1, w2
    fields[R_RANK], fields[R_RANK + 1] = rank1, rank2
    zero = jnp.zeros((1, tm), F32)
    route_t_ref[...] = jnp.concatenate([zero if f is None else f for f in fields], axis=0)


def _outproj_router(attn, rnn, x, w_out, g2, w_grp, w_exp, tm):
    N, D = x.shape
    w_out = w_out.astype(BF16)
    wrt = jnp.zeros((N_EXPERTS + SUBLANES, D), F32).at[:N_EXPERTS].set(w_exp.T)
    wrt = wrt.at[N_EXPERTS:N_EXPERTS + N_GROUPS].set(w_grp.T).astype(BF16)
    triu = jnp.triu(jnp.ones((tm, tm), BF16), 1)
    row = lambda i: (i, 0)
    const = lambda i: (0, 0)
    return pl.pallas_call(
        _outproj_kernel,
        grid=(N // tm,),
        in_specs=[
            pl.BlockSpec((tm, ATTN_WIDTH), row),
            pl.BlockSpec((tm, LRU_WIDTH), row),
            pl.BlockSpec((tm, D), row),
            pl.BlockSpec((ATTN_WIDTH, D), const),
            pl.BlockSpec((LRU_WIDTH, D), const),
            pl.BlockSpec((1, D), const),
            pl.BlockSpec(wrt.shape, const),
            pl.BlockSpec((tm, tm), const),
        ],
        out_specs=[pl.BlockSpec((tm, D), row), pl.BlockSpec((tm * ROW_SUB, LANES), row),
                   pl.BlockSpec((SUBLANES, tm), lambda i: (0, i)), pl.BlockSpec((N_EXPERTS, LANES), const)],
        out_shape=[jax.ShapeDtypeStruct((N, D), F32), jax.ShapeDtypeStruct((N * ROW_SUB, LANES), F32),
                   jax.ShapeDtypeStruct((SUBLANES, N), F32), jax.ShapeDtypeStruct((N_EXPERTS, LANES), F32)],
        scratch_shapes=[pltpu.VMEM((N_EXPERTS, LANES), F32)],
        compiler_params=_cparams(("arbitrary",)),
        name="outproj_router",
    )(attn, rnn, x, w_out[:ATTN_WIDTH], w_out[ATTN_WIDTH:], g2.reshape(1, D), wrt, triu)


def _dispatch_plan(route_t, counts, tg):
    N = route_t.shape[1]
    ids = route_t[R_ID:R_ID + 2].astype(jnp.int32)
    rank = route_t[R_RANK:R_RANK + 2].astype(jnp.int32)
    cnt = counts[:, 0].astype(jnp.int32)
    tiles_e = (cnt + tg - 1) // tg
    tile_end = jnp.cumsum(tiles_e)
    row_off = (tile_end - tiles_e) * tg
    pos = rank
    for e in range(N_EXPERTS):
        pos = pos + jnp.where(ids == e, row_off[e], 0)
    n_tiles = (2 * N) // tg + N_EXPERTS
    n_act = tile_end[-1]
    tile_id = jnp.minimum(jnp.arange(n_tiles, dtype=jnp.int32), n_act - 1)
    tile_expert = jnp.sum((tile_id[:, None] >= tile_end[None, :]).astype(jnp.int32), axis=1)
    pad_start = row_off + cnt
    pad_n = tiles_e * tg - cnt
    return pos[0], pos[1], tile_expert, n_act.reshape(1).astype(jnp.int32), pad_start, pad_n, n_tiles


def _row_copy(src, src_row, dst, dst_row, sem):
    return pltpu.make_async_copy(src.at[_tile_rows(src_row)], dst.at[_tile_rows(dst_row)], sem)


def _scatter_kernel(pad_start_ref, pad_n_ref, nact_ref, pos0_ref, pos1_ref, h_ref, xs_ref,
                    zero_ref, stage_ref, sem, sems, *, tg):
    tm = h_ref.shape[0] // ROW_SUB
    half = tg // 2

    @pl.when(pl.program_id(0) == 0)
    def _():
        zero_ref[...] = jnp.zeros_like(zero_ref)

        def fill_tile(j, c):
            cps = [pltpu.make_async_copy(zero_ref, xs_ref.at[_tile_rows(j * tg + h * half, half)], sem)
                   for h in range(2)]
            for cp in cps:
                cp.start()
            for cp in cps:
                cp.wait()
            return c

        lax.fori_loop(nact_ref[0], xs_ref.shape[0] // (tg * ROW_SUB), fill_tile, 0)

    @pl.when(pl.program_id(0) == 0)
    def _():
        for e in range(N_EXPERTS):
            n = pad_n_ref[e]
            off = pad_start_ref[e]
            b = 1
            while b < tg:
                hit = (n & b) != 0

                @pl.when(hit)
                def _(off=off, b=b):
                    cp = pltpu.make_async_copy(zero_ref.at[_tile_rows(0, b)], xs_ref.at[_tile_rows(off, b)], sem)
                    cp.start()
                    cp.wait()

                off = off + jnp.where(hit, b, 0)
                b *= 2

    step = pl.program_id(0)
    slot = step % 2
    stage = stage_ref.at[slot]
    stage[...] = h_ref[...]

    def issue(r, c):
        _row_copy(stage, r, xs_ref, pos0_ref[r], sems.at[slot]).start(priority=0)
        _row_copy(stage, r, xs_ref, pos1_ref[r], sems.at[slot]).start(priority=1)
        return c

    lax.fori_loop(0, tm, issue, 0, unroll=8)

    def drain(which):
        def body(r, c):
            _row_copy(stage, 0, xs_ref, 0, sems.at[which]).wait()
            _row_copy(stage, 0, xs_ref, 0, sems.at[which]).wait()
            return c

        lax.fori_loop(0, tm, body, 0, unroll=8)

    @pl.when(step > 0)
    def _():
        drain(1 - slot)

    @pl.when(step == pl.num_programs(0) - 1)
    def _():
        drain(slot)


def _scatter_rows(h2, pos0, pos1, pad_start, pad_n, n_act, n_rows, tm, tg):
    N = h2.shape[0] // ROW_SUB
    tok = lambda i, ps, pn, na: (i,)
    return pl.pallas_call(
        functools.partial(_scatter_kernel, tg=tg),
        grid_spec=pltpu.PrefetchScalarGridSpec(
            num_scalar_prefetch=3,
            grid=(N // tm,),
            in_specs=[
                pl.BlockSpec((tm,), tok, memory_space=pltpu.SMEM),
                pl.BlockSpec((tm,), tok, memory_space=pltpu.SMEM),
                pl.BlockSpec((tm * ROW_SUB, LANES), lambda i, ps, pn, na: (i, 0)),
            ],
            out_specs=pl.BlockSpec(memory_space=pl.ANY),
            scratch_shapes=[pltpu.VMEM((tg // 2 * ROW_SUB, LANES), F32), pltpu.VMEM((2, tm * ROW_SUB, LANES), F32),
                            pltpu.SemaphoreType.DMA, pltpu.SemaphoreType.DMA((2,))],
        ),
        out_shape=jax.ShapeDtypeStruct((n_rows * ROW_SUB, LANES), F32),
        compiler_params=_cparams(("arbitrary",)),
        name="moe_scatter",
    )(pad_start, pad_n, n_act, pos0, pos1, h2)


def _experts_kernel(te_ref, nact_ref, x_ref, wg_ref, wu_ref, wd_ref, y_ref, wg_s, wu_s, wd_s):
    j = pl.program_id(0)
    active = j < nact_ref[0]
    new_expert = (j == 0) | (te_ref[j] != te_ref[jnp.maximum(j - 1, 0)])

    @pl.when(active & new_expert)
    def _():
        wg_s[...] = wg_ref[...].astype(BF16)
        wu_s[...] = wu_ref[...].astype(BF16)
        wd_s[...] = wd_ref[...].astype(BF16)

    @pl.when(active)
    def _():
        x = _from_tiles(x_ref, x_ref.shape[0] // ROW_SUB).astype(BF16)
        gt = jnp.dot(x, wg_s[...], preferred_element_type=F32)
        up = jnp.dot(x, wu_s[...], preferred_element_type=F32)
        hid = (gt * jax.nn.sigmoid(gt)) * up
        _to_tiles(y_ref, jnp.dot(hid.astype(BF16), wd_s[...], preferred_element_type=F32))

    @pl.when(jnp.logical_not(active))
    def _():
        y_ref[...] = jnp.zeros_like(y_ref)


def _experts(xs, tile_expert, n_act, w_gate, w_up, w_down, tg):
    R, D = xs.shape[0] // ROW_SUB, D_MODEL
    rows = lambda j, te, na: (j, 0)
    wmap = lambda j, te, na: (te[j], 0, 0)
    return pl.pallas_call(
        _experts_kernel,
        grid_spec=pltpu.PrefetchScalarGridSpec(
            num_scalar_prefetch=2,
            grid=(R // tg,),
            in_specs=[
                pl.BlockSpec((tg * ROW_SUB, LANES), rows),
                pl.BlockSpec((None, D, D_EXPERT), wmap),
                pl.BlockSpec((None, D, D_EXPERT), wmap),
                pl.BlockSpec((None, D_EXPERT, D), wmap),
            ],
            out_specs=pl.BlockSpec((tg * ROW_SUB, LANES), rows),
            scratch_shapes=[pltpu.VMEM((D, D_EXPERT), BF16), pltpu.VMEM((D, D_EXPERT), BF16),
                            pltpu.VMEM((D_EXPERT, D), BF16)],
        ),
        out_shape=jax.ShapeDtypeStruct((R * ROW_SUB, LANES), F32),
        compiler_params=_cparams(("arbitrary",)),
        name="moe_experts",
    )(tile_expert, n_act, xs, w_gate, w_up, w_down)


def _combine_kernel(pos0_ref, pos1_ref, pos0n_ref, pos1n_ref, route_ref, x2_ref, fg_ref, ys_ref, o_ref,
                    buf_ref, sems):
    tm = x2_ref.shape[0]
    step = pl.program_id(0)
    slot = step % 2

    def issue(p0_ref, p1_ref, which):
        def body(r, c):
            _row_copy(ys_ref, p0_ref[r], buf_ref.at[which, 0], r, sems.at[which]).start(priority=0)
            _row_copy(ys_ref, p1_ref[r], buf_ref.at[which, 1], r, sems.at[which]).start(priority=1)
            return c

        lax.fori_loop(0, tm, body, 0, unroll=8)

    @pl.when(step == 0)
    def _():
        issue(pos0_ref, pos1_ref, slot)

    @pl.when(step + 1 < pl.num_programs(0))
    def _():
        issue(pos0n_ref, pos1n_ref, 1 - slot)

    def drain(r, c):
        _row_copy(ys_ref, 0, buf_ref.at[slot, 0], 0, sems.at[slot]).wait()
        _row_copy(ys_ref, 0, buf_ref.at[slot, 0], 0, sems.at[slot]).wait()
        return c

    lax.fori_loop(0, tm, drain, 0, unroll=8)
    route = route_ref[...].T
    w1 = route[:, R_W:R_W + 1]
    w2 = route[:, R_W + 1:R_W + 2]
    y = x2_ref[...] + w1 * _from_tiles(buf_ref.at[slot, 0], tm) + w2 * _from_tiles(buf_ref.at[slot, 1], tm)
    ms = jnp.mean(y * y, axis=-1, keepdims=True)
    o_ref[...] = y * lax.rsqrt(ms + EPS) * fg_ref[...]


def _combine(ys, pos0, pos1, route, x2, final_g, tm):
    N, D = x2.shape
    n = N // tm
    row = lambda i: (i, 0)
    cur = lambda i: (i,)
    nxt = lambda i: (jnp.minimum(i + 1, n - 1),)
    smem = functools.partial(pl.BlockSpec, (tm,), memory_space=pltpu.SMEM)
    return pl.pallas_call(
        _combine_kernel,
        grid=(n,),
        in_specs=[
            smem(cur), smem(cur), smem(nxt), smem(nxt),
            pl.BlockSpec((SUBLANES, tm), lambda i: (0, i)),
            pl.BlockSpec((tm, D), row),
            pl.BlockSpec((1, D), lambda i: (0, 0)),
            pl.BlockSpec(memory_space=pl.ANY),
        ],
        out_specs=pl.BlockSpec((tm, D), row),
        out_shape=jax.ShapeDtypeStruct((N, D), F32),
        scratch_shapes=[pltpu.VMEM((2, 2, tm * ROW_SUB, LANES), F32), pltpu.SemaphoreType.DMA((2,))],
        compiler_params=_cparams(("arbitrary",)),
        name="moe_combine",
    )(pos0, pos1, pos0, pos1, route, x2, final_g.reshape(1, D), ys)


def _tiles(B, S):
    return dict(
        tm_in=min(512, S),
        tq=min(1024, S),
        kc=min(256, S),
        tcp=min(32, S // 16),
        tm_out=min(512, S),
        tm_row=min(512, S),
        tg=min(512, S),
    )


def kernel(x, norm1_g, w_in, lambda_q1, lambda_k1, lambda_q2, lambda_k2, subln_g, conv_w, conv_b,
           lru_w_r, lru_b_r, lru_w_i, lru_b_i, lru_lambda, w_out, norm2_g, w_grp, w_exp, w_gate, w_up,
           w_down, final_g):
    B, S, D = x.shape
    t = _tiles(B, S)
    l = 0
    q, k, vt, xr, gate = _inproj(x, norm1_g[l], w_in[l], t["tm_in"])
    lam_vecs = jnp.zeros((SUBLANES, LANES), F32).at[0:4, :HEAD_DIM].set(
        jnp.stack([lambda_q1[l], lambda_k1[l], lambda_q2[l], lambda_k2[l]]))
    attn = _attention(q, k, vt, lam_vecs, subln_g[l], t["tq"], t["kc"])
    rnn = _rglru(xr, gate, conv_w[l], conv_b[l], lru_w_r[l], lru_b_r[l], lru_w_i[l], lru_b_i[l],
                 lru_lambda[l], t["tcp"])
    N = B * S
    x2, h2, route_t, counts = _outproj_router(
        attn.reshape(N, ATTN_WIDTH), rnn.reshape(N, LRU_WIDTH), x.reshape(N, D), w_out[l], norm2_g[l],
        w_grp[l], w_exp[l], t["tm_out"])
    tg = t["tg"]
    pos0, pos1, tile_expert, n_act, pad_start, pad_n, n_tiles = _dispatch_plan(route_t, counts, tg)
    xs = _scatter_rows(h2, pos0, pos1, pad_start, pad_n, n_act, n_tiles * tg, t["tm_row"], tg)
    ys = _experts(xs, tile_expert, n_act, w_gate[l], w_up[l], w_down[l], tg)
    out = _combine(ys, pos0, pos1, route_t, x2, final_g, t["tm_row"])
    return out.reshape(B, S, D)
```
